```python
import jax
import jax.numpy as jnp
from jax import lax
import numpy as np

D_MODEL = 1024
BATCH = 16
SEQ = 2048
DEPTH = 1
DEC_BATCH = 128
DEC_SEQ = 8
PAST_LEN = 8192
PAGE_SIZE = 128

HEAD_DIM = 64
N_HEADS = D_MODEL // HEAD_DIM
N_HEADS_A = N_HEADS // 2
N_HEADS_B = N_HEADS - N_HEADS_A
N_KV_B = 2
GROUP_B = N_HEADS_B // N_KV_B
WIDTH_A = N_HEADS_A * HEAD_DIM
WIDTH_B = N_HEADS_B * HEAD_DIM
KV_WIDTH_B = N_KV_B * HEAD_DIM
ROPE_DIM = HEAD_DIM // 4
ROPE_THETA = 500000.0
L_CMP = 32
L_SEL = 64
N_SEL = 16
WINDOW = 512
QBLOCK = 128
D_FF = 4 * D_MODEL
N_BRANCH = 3
EPS = 1e-6
NEG = -1e30
SEL_BONUS = 1e4
F32 = jnp.float32
PROJ_SIZES = (WIDTH_A, WIDTH_A, WIDTH_A, WIDTH_B) + (KV_WIDTH_B,) * 6 + (N_BRANCH * N_HEADS_B,)
PROJ_WIDTH = sum(PROJ_SIZES)

kernel_name = "hybrid_stickbreak_nsa_step"


def rmsnorm(x, g):
    xf = x.astype(F32)
    y = xf * lax.rsqrt(jnp.mean(xf * xf, axis=-1, keepdims=True) + EPS)
    return (y * g.astype(F32)).astype(x.dtype)


def rope(x, pos):
    half = ROPE_DIM // 2
    inv = ROPE_THETA ** (-jnp.arange(half, dtype=F32) / half)
    ang = pos.astype(F32)[:, None] * inv[None, :]
    cos = jnp.cos(ang)[:, None, :]
    sin = jnp.sin(ang)[:, None, :]
    xr = x[..., :ROPE_DIM].astype(F32)
    x1, x2 = xr[..., :half], xr[..., half:]
    rot = jnp.concatenate([x1 * cos - x2 * sin, x2 * cos + x1 * sin], axis=-1).astype(x.dtype)
    return jnp.concatenate([rot, x[..., ROPE_DIM:]], axis=-1)


def qblock(t):
    return QBLOCK if t % QBLOCK == 0 else t


def project(h, pos, w_in, b_gate, q_norm_b, k_norm_b):
    B, T, _ = h.shape
    p = h @ w_in
    offs = [int(o) for o in np.cumsum(PROJ_SIZES)[:-1]]
    parts = jnp.split(p, offs, axis=-1)
    qa, ka, va = (t.reshape(B, T, N_HEADS_A, HEAD_DIM) for t in parts[0:3])
    qb = parts[3].reshape(B, T, N_HEADS_B, HEAD_DIM)
    kc, vc, ks, vs, kw, vw = (t.reshape(B, T, N_KV_B, HEAD_DIM) for t in parts[4:10])
    gates = jax.nn.sigmoid((parts[10] + b_gate).astype(F32)).reshape(B, T, N_BRANCH, N_HEADS_B)
    qb = rope(rmsnorm(qb, q_norm_b), pos)
    kc = rope(rmsnorm(kc, k_norm_b[0]), pos)
    ks = rope(rmsnorm(ks, k_norm_b[1]), pos)
    kw = rope(rmsnorm(kw, k_norm_b[2]), pos)
    sba_kv = jnp.stack([ka, va], axis=2)
    nsa_kv = jnp.stack([kc, vc, ks, vs], axis=2)
    win_kv = jnp.stack([kw, vw], axis=2)
    return qa, qb, gates, sba_kv, nsa_kv, win_kv


def stick_breaking(z, valid):
    log_beta = jax.nn.log_sigmoid(z)
    log_rest = jnp.where(valid, jax.nn.log_sigmoid(-z), 0.0)
    after = lax.cumsum(log_rest, axis=z.ndim - 1, reverse=True) - log_rest
    return jnp.where(valid, jnp.exp(log_beta + after), 0.0)


def sba_prompt(q, k, v):
    B, T = q.shape[:2]
    qb = qblock(T)
    nq = T // qb
    scale = HEAD_DIM ** -0.5
    qs = q.reshape(B, nq, qb, N_HEADS_A, HEAD_DIM).transpose(1, 0, 2, 3, 4)
    kpos = jnp.arange(T)

    def blk(args):
        qi, i = args
        z = jnp.einsum('bqhd,bshd->bhqs', qi, k).astype(F32) * scale
        tpos = i * qb + jnp.arange(qb)
        a = stick_breaking(z, kpos[None, :] < tpos[:, None])
        return jnp.einsum('bhqs,bshd->bqhd', a.astype(v.dtype), v)

    o = lax.map(blk, (qs, jnp.arange(nq)))
    return o.transpose(1, 0, 2, 3, 4).reshape(B, T, N_HEADS_A, HEAD_DIM)


def sba_sample(q, k_new, v_new, k_past, v_past):
    T = q.shape[1]
    P = k_past.shape[1]
    scale = HEAD_DIM ** -0.5
    z = jnp.concatenate([jnp.einsum('bqhd,bshd->bhqs', q, k_past),
                         jnp.einsum('bqhd,bshd->bhqs', q, k_new)], axis=-1).astype(F32) * scale
    tpos = jnp.arange(T)
    valid = jnp.concatenate([jnp.ones((T, P), dtype=bool), tpos[None, :] < tpos[:, None]], axis=-1)
    a = stick_breaking(z, valid).astype(v_new.dtype)
    return (jnp.einsum('bhqs,bshd->bqhd', a[..., :P], v_past)
            + jnp.einsum('bhqs,bshd->bqhd', a[..., P:], v_new))


def nsa_cmp_sel(q, q_pos, kc, vc, ks, vs, w_cmp_k, w_cmp_v):
    B, T = q.shape[:2]
    L = kc.shape[1]
    scale = HEAD_DIM ** -0.5
    qg = q.reshape(B, T, N_KV_B, GROUP_B, HEAD_DIM)
    ncb = L // L_CMP

    def compress(x, w):
        xb = x[:, :ncb * L_CMP].reshape(B, ncb, L_CMP, N_KV_B, HEAD_DIM)
        xb = xb.transpose(0, 1, 3, 2, 4).reshape(B, ncb, N_KV_B, L_CMP * HEAD_DIM)
        return xb @ w

    kcmp = compress(kc, w_cmp_k)
    vcmp = compress(vc, w_cmp_v)
    zc = jnp.einsum('btgmd,bngd->bgmtn', qg, kcmp).astype(F32) * scale
    blk_end = (jnp.arange(ncb) + 1) * L_CMP - 1
    cvalid = blk_end[None, :] <= q_pos[:, None]
    pc = jax.nn.softmax(jnp.where(cvalid, zc, NEG), axis=-1) * cvalid
    o_cmp = jnp.einsum('bgmtn,bngd->btgmd', pc.astype(vc.dtype), vcmp).reshape(B, T, N_HEADS_B, HEAD_DIM)
    nsb = -(-L // L_SEL)
    ratio = L_SEL // L_CMP
    imp = pc.sum(axis=2)
    imp = jnp.pad(imp, ((0, 0), (0, 0), (0, 0), (0, nsb * ratio - ncb)))
    imp = imp.reshape(B, N_KV_B, T, nsb, ratio).sum(-1)
    sb = jnp.arange(nsb)
    imp = jnp.where(sb[None, :] * L_SEL > q_pos[:, None], NEG, imp)
    forced = (sb[None, :] == 0) | (sb[None, :] == (q_pos // L_SEL)[:, None])
    imp = jnp.where(forced, SEL_BONUS, imp)
    n_sel = min(N_SEL, nsb)
    _, idx = lax.top_k(imp, n_sel)
    pad = nsb * L_SEL - L

    def to_blocks(x):
        x = jnp.pad(x, ((0, 0), (0, pad), (0, 0), (0, 0)))
        return x.reshape(B, nsb, L_SEL, N_KV_B, HEAD_DIM).transpose(0, 3, 1, 2, 4)

    ksb = to_blocks(ks)
    vsb = to_blocks(vs)
    gather = jax.vmap(jax.vmap(lambda xb, ix: xb[ix]))
    qbs = qblock(T)
    nq = T // qbs
    q_ch = qg.reshape(B, nq, qbs, N_KV_B, GROUP_B, HEAD_DIM).transpose(1, 0, 2, 3, 4, 5)
    idx_ch = idx.reshape(B, N_KV_B, nq, qbs, n_sel).transpose(2, 0, 1, 3, 4)
    pos_ch = q_pos.reshape(nq, qbs)

    def sel_block(args):
        qi, ix, pi = args
        kg = gather(ksb, ix)
        vg = gather(vsb, ix)
        kpos = ix[..., None] * L_SEL + jnp.arange(L_SEL)
        valid = kpos <= pi[None, None, :, None, None]
        z = jnp.einsum('bqgmd,bgqnld->bgmqnl', qi, kg).astype(F32) * scale
        p = jax.nn.softmax(jnp.where(valid[:, :, None], z, NEG), axis=(-2, -1))
        return jnp.einsum('bgmqnl,bgqnld->bqgmd', p.astype(vg.dtype), vg)

    o_sel = lax.map(sel_block, (q_ch, idx_ch, pos_ch))
    o_sel = o_sel.transpose(1, 0, 2, 3, 4, 5).reshape(B, T, N_HEADS_B, HEAD_DIM)
    return o_cmp, o_sel


def window_attend(q, q_pos, k, v, k_pos):
    B, Tq = q.shape[:2]
    scale = HEAD_DIM ** -0.5
    qg = q.reshape(B, Tq, N_KV_B, GROUP_B, HEAD_DIM)
    z = jnp.einsum('bqgmd,bsgd->bgmqs', qg, k).astype(F32) * scale
    kp = k_pos[None, :]
    tp = q_pos[:, None]
    valid = (kp >= 0) & (kp <= tp) & (kp > tp - WINDOW)
    p = jax.nn.softmax(jnp.where(valid, z, NEG), axis=-1)
    o = jnp.einsum('bgmqs,bsgd->bqgmd', p.astype(v.dtype), v)
    return o.reshape(B, Tq, N_HEADS_B, HEAD_DIM)


def window_prompt(q, kw, vw):
    B, T = q.shape[:2]
    qb = qblock(T)
    nq = T // qb
    span = WINDOW + qb
    kp = jnp.pad(kw, ((0, 0), (WINDOW, 0), (0, 0), (0, 0)))
    vp = jnp.pad(vw, ((0, 0), (WINDOW, 0), (0, 0), (0, 0)))
    qs = q.reshape(B, nq, qb, N_HEADS_B, HEAD_DIM).transpose(1, 0, 2, 3, 4)

    def blk(args):
        qi, i = args
        start = i * qb
        kb = lax.dynamic_slice_in_dim(kp, start, span, axis=1)
        vb = lax.dynamic_slice_in_dim(vp, start, span, axis=1)
        return window_attend(qi, start + jnp.arange(qb), kb, vb, start - WINDOW + jnp.arange(span))

    o = lax.map(blk, (qs, jnp.arange(nq)))
    return o.transpose(1, 0, 2, 3, 4).reshape(B, T, N_HEADS_B, HEAD_DIM)


def combine(x, o_a, o_c, o_s, o_w, gates, norm_out_a, norm_out_b, w_out, norm_mlp, w_up, w_down):
    B, T, _ = x.shape
    g = gates.astype(o_c.dtype)[..., None]
    o_b = g[:, :, 0] * o_c + g[:, :, 1] * o_s + g[:, :, 2] * o_w
    mixed = jnp.concatenate([rmsnorm(o_a.reshape(B, T, WIDTH_A), norm_out_a),
                             rmsnorm(o_b.reshape(B, T, WIDTH_B), norm_out_b)], axis=-1)
    x = x + mixed @ w_out
    h = rmsnorm(x, norm_mlp)
    return x + jnp.square(jax.nn.relu(h @ w_up)) @ w_down


def layer_prompt(x, norm_attn, w_in, b_gate, q_norm_b, k_norm_b, w_cmp_k, w_cmp_v,
                 norm_out_a, norm_out_b, w_out, norm_mlp, w_up, w_down):
    T = x.shape[1]
    pos = jnp.arange(T)
    h = rmsnorm(x, norm_attn)
    qa, qb, gates, sba_kv, nsa_kv, win_kv = project(h, pos, w_in, b_gate, q_norm_b, k_norm_b)
    o_a = sba_prompt(qa, sba_kv[:, :, 0], sba_kv[:, :, 1])
    o_c, o_s = nsa_cmp_sel(qb, pos, nsa_kv[:, :, 0], nsa_kv[:, :, 1], nsa_kv[:, :, 2], nsa_kv[:, :, 3],
                           w_cmp_k, w_cmp_v)
    o_w = window_prompt(qb, win_kv[:, :, 0], win_kv[:, :, 1])
    y = combine(x, o_a, o_c, o_s, o_w, gates, norm_out_a, norm_out_b, w_out, norm_mlp, w_up, w_down)
    return y, sba_kv, nsa_kv, win_kv[:, T - min(WINDOW, T):]


def layer_sample(x, sba_past, nsa_past, win_buf, norm_attn, w_in, b_gate, q_norm_b, k_norm_b,
                 w_cmp_k, w_cmp_v, norm_out_a, norm_out_b, w_out, norm_mlp, w_up, w_down):
    T = x.shape[1]
    P = sba_past.shape[1]
    W = win_buf.shape[1]
    pos = P + jnp.arange(T)
    h = rmsnorm(x, norm_attn)
    qa, qb, gates, sba_kv, nsa_kv, win_kv = project(h, pos, w_in, b_gate, q_norm_b, k_norm_b)
    o_a = sba_sample(qa, sba_kv[:, :, 0], sba_kv[:, :, 1], sba_past[:, :, 0], sba_past[:, :, 1])
    nsa_all = jnp.concatenate([nsa_past, nsa_kv], axis=1)
    o_c, o_s = nsa_cmp_sel(qb, pos, nsa_all[:, :, 0], nsa_all[:, :, 1], nsa_all[:, :, 2], nsa_all[:, :, 3],
                           w_cmp_k, w_cmp_v)
    win_all = jnp.concatenate([win_buf, win_kv], axis=1)
    o_w = window_attend(qb, pos, win_all[:, :, 0], win_all[:, :, 1], P - W + jnp.arange(W + T))
    y = combine(x, o_a, o_c, o_s, o_w, gates, norm_out_a, norm_out_b, w_out, norm_mlp, w_up, w_down)
    return y, sba_kv, nsa_kv, win_all[:, T:]


def setup_inputs(seed: int = 0) -> dict:
    key = jax.random.key(seed)
    ks = jax.random.split(key, 24)
    n_pages = PAST_LEN // PAGE_SIZE
    n_used = DEC_BATCH * n_pages
    n_pool = n_used + n_used // 4
    w_len = min(WINDOW, PAST_LEN)
    nrm = lambda k, s: jax.random.normal(k, s, F32)
    page_table = jax.random.permutation(ks[0], n_pool)[:n_used].reshape(DEC_BATCH, n_pages).astype(jnp.int32)
    return {
        "x_prompt": nrm(ks[1], (BATCH, SEQ, D_MODEL)),
        "x_sample": nrm(ks[2], (DEC_BATCH, DEC_SEQ, D_MODEL)),
        "cache_sba_kv": nrm(ks[3], (DEPTH, n_pool, PAGE_SIZE, 2, N_HEADS_A, HEAD_DIM)),
        "cache_nsa_kv": nrm(ks[4], (DEPTH, n_pool, PAGE_SIZE, 4, N_KV_B, HEAD_DIM)),
        "state_win_kv": nrm(ks[5], (DEPTH, DEC_BATCH, w_len, 2, N_KV_B, HEAD_DIM)),
        "page_table": page_table,
        "norm_attn": 1.0 + 0.02 * nrm(ks[6], (DEPTH, D_MODEL)),
        "w_in": nrm(ks[7], (DEPTH, D_MODEL, PROJ_WIDTH)) * D_MODEL ** -0.5,
        "b_gate": 0.02 * nrm(ks[8], (DEPTH, N_BRANCH * N_HEADS_B)),
        "q_norm_b": 1.0 + 0.02 * nrm(ks[9], (DEPTH, HEAD_DIM)),
        "k_norm_b": 1.0 + 0.02 * nrm(ks[10], (DEPTH, N_BRANCH, HEAD_DIM)),
        "w_cmp_k": nrm(ks[11], (DEPTH, L_CMP * HEAD_DIM, HEAD_DIM)) * (L_CMP * HEAD_DIM) ** -0.5,
        "w_cmp_v": nrm(ks[12], (DEPTH, L_CMP * HEAD_DIM, HEAD_DIM)) * (L_CMP * HEAD_DIM) ** -0.5,
        "norm_out_a": 1.0 + 0.02 * nrm(ks[13], (DEPTH, WIDTH_A)),
        "norm_out_b": 1.0 + 0.02 * nrm(ks[14], (DEPTH, WIDTH_B)),
        "w_out": nrm(ks[15], (DEPTH, D_MODEL, D_MODEL)) * D_MODEL ** -0.5,
        "norm_mlp": 1.0 + 0.02 * nrm(ks[16], (DEPTH, D_MODEL)),
        "w_up": nrm(ks[17], (DEPTH, D_MODEL, D_FF)) * D_MODEL ** -0.5,
        "w_down": nrm(ks[18], (DEPTH, D_FF, D_MODEL)) * D_FF ** -0.5,
    }


def reference(x_prompt, x_sample, cache_sba_kv, cache_nsa_kv, state_win_kv, page_table,
              norm_attn, w_in, b_gate, q_norm_b, k_norm_b, w_cmp_k, w_cmp_v,
              norm_out_a, norm_out_b, w_out, norm_mlp, w_up, w_down):
    db, n_pages = page_table.shape
    page = cache_sba_kv.shape[2]
    past = n_pages * page
    hp, hs = x_prompt, x_sample
    sba_p, nsa_p, win_p, sba_s, nsa_s, win_s = [], [], [], [], [], []
    for l in range(DEPTH):
        lw = (norm_attn[l], w_in[l], b_gate[l], q_norm_b[l], k_norm_b[l], w_cmp_k[l], w_cmp_v[l],
              norm_out_a[l], norm_out_b[l], w_out[l], norm_mlp[l], w_up[l], w_down[l])
        hp, a_p, n_p, w_p = layer_prompt(hp, *lw)
        sba_past = cache_sba_kv[l][page_table].reshape(db, past, 2, N_HEADS_A, HEAD_DIM)
        nsa_past = cache_nsa_kv[l][page_table].reshape(db, past, 4, N_KV_B, HEAD_DIM)
        hs, a_s, n_s, w_s = layer_sample(hs, sba_past, nsa_past, state_win_kv[l], *lw)
        sba_p.append(a_p)
        nsa_p.append(n_p)
        win_p.append(w_p)
        sba_s.append(a_s)
        nsa_s.append(n_s)
        win_s.append(w_s)
    return (hp, hs, jnp.stack(sba_p), jnp.stack(nsa_p), jnp.stack(win_p),
            jnp.stack(sba_s), jnp.stack(nsa_s), jnp.stack(win_s))
```

```python
import functools

import jax
import jax.numpy as jnp
import numpy as np
from jax import lax
from jax.experimental import pallas as pl
from jax.experimental.pallas import tpu as pltpu

F32 = jnp.float32
BF16 = jnp.bfloat16
I32 = jnp.int32

HEAD_DIM = 64
N_HEADS_A = 8
N_HEADS_B = 8
N_KV_B = 2
GROUP_B = N_HEADS_B // N_KV_B
WIDTH_A = N_HEADS_A * HEAD_DIM
WIDTH_B = N_HEADS_B * HEAD_DIM
KV_WIDTH_B = N_KV_B * HEAD_DIM
ROPE_DIM = HEAD_DIM // 4
ROPE_HALF = ROPE_DIM // 2
ROPE_THETA = 500000.0
L_CMP = 32
L_SEL = 64
CMP_PER_SEL = L_SEL // L_CMP
N_SEL = 16
WINDOW = 512
QBLOCK = 128
N_BRANCH = 3
EPS = 1e-6
NEG = -1e30
SEL_BONUS = 1e4
SCALE = HEAD_DIM ** -0.5

LANES = 128
PAGES_PER_STEP = 8
VMEM_LIMIT = 56 * 1024 * 1024

OFF_QA = 0
OFF_SBA = WIDTH_A
OFF_QB = 3 * WIDTH_A
OFF_NSA = OFF_QB + WIDTH_B
OFF_WIN = OFF_NSA + 4 * KV_WIDTH_B
OFF_GATE = OFF_WIN + 2 * KV_WIDTH_B
N_GATE = N_BRANCH * N_HEADS_B
PROJ_ROWS = OFF_GATE + LANES


def _nt(a, b):
    return lax.dot_general(a, b, (((1,), (1,)), ((), ())), preferred_element_type=F32)


def _tn(a, b):
    return lax.dot_general(a, b, (((0,), (0,)), ((), ())), preferred_element_type=F32)


def _nn(a, b):
    return jnp.dot(a, b, preferred_element_type=F32)


def _split_dot(x, u):
    hi = x.astype(BF16)
    lo = (x - hi.astype(F32)).astype(BF16)
    return _nn(hi, u) + _nn(lo, u)


def _params(*sem):
    return pltpu.CompilerParams(dimension_semantics=sem, vmem_limit_bytes=VMEM_LIMIT)


def _const_spec(shape):
    nd = len(shape)
    return pl.BlockSpec(shape, lambda *_: (0,) * nd, pipeline_mode=pl.Buffered(1))


def _norm_rope_t(xt, gcol, cos, sin):
    ms = jnp.mean(xt * xt, axis=0, keepdims=True)
    y = xt * lax.rsqrt(ms + EPS) * gcol
    x1 = y[0:ROPE_HALF]
    x2 = y[ROPE_HALF:ROPE_DIM]
    return jnp.concatenate([x1 * cos - x2 * sin, x2 * cos + x1 * sin, y[ROPE_DIM:]], axis=0)


def _proj_kernel(x_ref, g_ref, wt_ref, bg_ref, qn_ref, kn_ref, cos_ref, sin_ref,
                 qat_ref, sbat_ref, qbt_ref, nsat_ref, wint_ref, gt_ref, cmpin_ref):
    x = x_ref[0]
    ms = jnp.mean(x * x, axis=-1, keepdims=True)
    h = (x * lax.rsqrt(ms + EPS)) * g_ref[...]
    pt = _nt(wt_ref[...], h.astype(BF16))
    cos = cos_ref[...]
    sin = sin_ref[...]
    qat_ref[0] = pt[OFF_QA:OFF_QA + WIDTH_A].astype(BF16)
    sbat_ref[0] = pt[OFF_SBA:OFF_SBA + 2 * WIDTH_A]
    qn = qn_ref[...]
    for hd in range(N_HEADS_B):
        lo = OFF_QB + hd * HEAD_DIM
        qbt_ref[0, hd * HEAD_DIM:(hd + 1) * HEAD_DIM] = _norm_rope_t(pt[lo:lo + HEAD_DIM], qn, cos, sin).astype(BF16)
    cmp_rows = []
    for slot in range(4 * N_KV_B):
        lo = OFF_NSA + slot * HEAD_DIM
        blk = pt[lo:lo + HEAD_DIM]
        kind = slot // N_KV_B
        if kind == 0:
            blk = _norm_rope_t(blk, kn_ref[0], cos, sin)
        elif kind == 2:
            blk = _norm_rope_t(blk, kn_ref[1], cos, sin)
        if kind < 2:
            cmp_rows.append(blk)
        nsat_ref[0, slot * HEAD_DIM:(slot + 1) * HEAD_DIM] = blk
    for slot in range(2 * N_KV_B):
        lo = OFF_WIN + slot * HEAD_DIM
        blk = pt[lo:lo + HEAD_DIM]
        if slot // N_KV_B == 0:
            blk = _norm_rope_t(blk, kn_ref[2], cos, sin)
        wint_ref[0, slot * HEAD_DIM:(slot + 1) * HEAD_DIM] = blk
    gt_ref[0] = jax.nn.sigmoid(pt[OFF_GATE:OFF_GATE + LANES] + bg_ref[...])
    cmpin_ref[0] = jnp.concatenate(cmp_rows, axis=0).T.astype(BF16)


def _project(x, pos, norm_attn, wt, bg_col, qn_col, kn_cols, tile):
    nb, t, d = x.shape
    half = ROPE_HALF
    inv = ROPE_THETA ** (-jnp.arange(half, dtype=F32) / half)
    ang = inv[:, None] * pos.astype(F32)[None, :]
    cos = jnp.cos(ang)
    sin = jnp.sin(ang)
    grid = (nb, t // tile)
    row = lambda n: pl.BlockSpec((1, n, tile), lambda b, i: (b, 0, i))
    out_shape = (
        jax.ShapeDtypeStruct((nb, WIDTH_A, t), BF16),
        jax.ShapeDtypeStruct((nb, 2 * WIDTH_A, t), F32),
        jax.ShapeDtypeStruct((nb, WIDTH_B, t), BF16),
        jax.ShapeDtypeStruct((nb, 4 * KV_WIDTH_B, t), F32),
        jax.ShapeDtypeStruct((nb, 2 * KV_WIDTH_B, t), F32),
        jax.ShapeDtypeStruct((nb, LANES, t), F32),
        jax.ShapeDtypeStruct((nb, t, 2 * KV_WIDTH_B), BF16),
    )
    return pl.pallas_call(
        _proj_kernel,
        grid=grid,
        in_specs=[
            pl.BlockSpec((1, tile, d), lambda b, i: (b, i, 0)),
            _const_spec((1, d)),
            _const_spec((PROJ_ROWS, d)),
            _const_spec((LANES, 1)),
            _const_spec((HEAD_DIM, 1)),
            _const_spec((N_BRANCH, HEAD_DIM, 1)),
            pl.BlockSpec((half, tile), lambda b, i: (0, i)),
            pl.BlockSpec((half, tile), lambda b, i: (0, i)),
        ],
        out_specs=(row(WIDTH_A), row(2 * WIDTH_A), row(WIDTH_B), row(4 * KV_WIDTH_B), row(2 * KV_WIDTH_B),
                   row(LANES), pl.BlockSpec((1, tile, 2 * KV_WIDTH_B), lambda b, i: (b, i, 0))),
        out_shape=out_shape,
        compiler_params=_params("parallel", "parallel"),
        name="project",
    )(x, norm_attn, wt, bg_col, qn_col, kn_cols, cos, sin)


def _compress_kernel(x_ref, w_ref, o_ref):
    o_ref[...] = _nn(x_ref[...], w_ref[...])


def _compress(xflat, wexp):
    rows, k = xflat.shape
    rb = min(rows, 256)
    return pl.pallas_call(
        _compress_kernel,
        grid=(rows // rb,),
        in_specs=[pl.BlockSpec((rb, k), lambda i: (i, 0)), _const_spec(wexp.shape)],
        out_specs=pl.BlockSpec((rb, wexp.shape[1]), lambda i: (i, 0)),
        out_shape=jax.ShapeDtypeStruct((rows, wexp.shape[1]), F32),
        compiler_params=_params("parallel"),
        name="compress",
    )(xflat, wexp)


def _stick_block(z, c, u, valid):
    lb = jnp.minimum(z, 0.0) - jnp.log1p(jnp.exp(-jnp.abs(z)))
    lr = lb - z
    if valid is not None:
        lr = jnp.where(valid, lr, 0.0)
    after = _split_dot(lr, u)
    a = jnp.exp(lb + after + c)
    if valid is not None:
        a = jnp.where(valid, a, 0.0)
    return a, c + after[:, 0:1] + lr[:, 0:1]


def _sba_prompt_kernel(q_ref, k_ref, v_ref, u_ref, o_ref):
    qi = pl.program_id(2)
    q = q_ref[0].astype(F32).T.astype(BF16)
    u = u_ref[...]
    row = lax.broadcasted_iota(I32, (QBLOCK, QBLOCK), 0)
    col = lax.broadcasted_iota(I32, (QBLOCK, QBLOCK), 1)

    def block(j, carry, valid):
        c, acc = carry
        off = pl.multiple_of(j * QBLOCK, QBLOCK)
        kt = k_ref[0, :, pl.ds(off, QBLOCK)].astype(BF16)
        vt = v_ref[0, :, pl.ds(off, QBLOCK)].astype(BF16)
        z = _nn(q, kt) * SCALE
        a, c = _stick_block(z, c, u, valid)
        return c, acc + _nt(vt, a.astype(BF16))

    carry = (jnp.zeros((QBLOCK, 1), F32), jnp.zeros((HEAD_DIM, QBLOCK), F32))
    carry = block(qi, carry, col < row)
    carry = lax.fori_loop(0, qi, lambda i, cr: block(qi - 1 - i, cr, None), carry)
    o_ref[0] = carry[1]


def _sba_prompt(qat, sbat, u):
    nb, _, t = qat.shape
    nq = t // QBLOCK
    return pl.pallas_call(
        _sba_prompt_kernel,
        grid=(nb, N_HEADS_A, nq),
        in_specs=[
            pl.BlockSpec((1, HEAD_DIM, QBLOCK), lambda b, h, i: (b, h, i)),
            pl.BlockSpec((1, HEAD_DIM, t), lambda b, h, i: (b, h, 0)),
            pl.BlockSpec((1, HEAD_DIM, t), lambda b, h, i: (b, N_HEADS_A + h, 0)),
            _const_spec((QBLOCK, QBLOCK)),
        ],
        out_specs=pl.BlockSpec((1, HEAD_DIM, QBLOCK), lambda b, h, i: (b, h, i)),
        out_shape=jax.ShapeDtypeStruct((nb, WIDTH_A, t), F32),
        compiler_params=_params("parallel", "parallel", "arbitrary"),
        name="sba_prompt",
    )(qat, sbat, sbat, u)


def _sba_sample_kernel(pt_ref, q_ref, new_ref, *rest, n_steps):
    page_refs = rest[:PAGES_PER_STEP]
    u_ref, o_ref, c_ref, acc_ref, qbd_ref = rest[PAGES_PER_STEP:]
    jj = pl.program_id(1)
    t_new = q_ref.shape[1]
    rows = N_HEADS_A * t_new
    u = u_ref[...]

    def process(ref, valid):
        kt = ref[0, 0].reshape(WIDTH_A, LANES).astype(BF16)
        vt = ref[0, 1].reshape(WIDTH_A, LANES).astype(BF16)
        z = _nn(qbd_ref[...], kt) * SCALE
        a, c = _stick_block(z, c_ref[...], u, valid)
        c_ref[...] = c
        acc_ref[...] += _nt(a.astype(BF16), vt)

    @pl.when(jj == 0)
    def _():
        q = jnp.concatenate([q_ref[0].astype(F32)] * N_HEADS_A, axis=0)
        r = lax.broadcasted_iota(I32, q.shape, 0) // t_new
        cidx = lax.broadcasted_iota(I32, q.shape, 1) // HEAD_DIM
        qbd_ref[...] = jnp.where(r == cidx, q, 0.0).astype(BF16)
        c_ref[...] = jnp.zeros_like(c_ref)
        acc_ref[...] = jnp.zeros_like(acc_ref)
        tq = lax.broadcasted_iota(I32, (rows, LANES), 0) % t_new
        s = lax.broadcasted_iota(I32, (rows, LANES), 1)
        process(new_ref, s < tq)

    for ref in page_refs:
        process(ref, None)

    @pl.when(jj == n_steps - 1)
    def _():
        acc = acc_ref[...]
        r = lax.broadcasted_iota(I32, acc.shape, 0) // t_new
        cidx = lax.broadcasted_iota(I32, acc.shape, 1) // HEAD_DIM
        acc = jnp.where(r == cidx, acc, 0.0)
        out = acc[0:t_new]
        for hd in range(1, N_HEADS_A):
            out = out + acc[hd * t_new:(hd + 1) * t_new]
        o_ref[0] = out


def _sba_sample(page_table, q, newt, cache_t, u):
    nb, t_new, _ = q.shape
    n_pages = page_table.shape[1]
    n_steps = n_pages // PAGES_PER_STEP
    rows = N_HEADS_A * t_new

    def page_spec(i):
        def imap(b, j, pt):
            return (pt[b, n_pages - 1 - (j * PAGES_PER_STEP + i)], 0, 0, 0, 0)
        return pl.BlockSpec((1, 2, N_HEADS_A, HEAD_DIM, LANES), imap)

    grid_spec = pltpu.PrefetchScalarGridSpec(
        num_scalar_prefetch=1,
        grid=(nb, n_steps),
        in_specs=[
            pl.BlockSpec((1, t_new, WIDTH_A), lambda b, j, pt: (b, 0, 0)),
            pl.BlockSpec((1, 2, N_HEADS_A, HEAD_DIM, LANES), lambda b, j, pt: (b, 0, 0, 0, 0)),
            *[page_spec(i) for i in range(PAGES_PER_STEP)],
            _const_spec((LANES, LANES)),
        ],
        out_specs=pl.BlockSpec((1, t_new, WIDTH_A), lambda b, j, pt: (b, 0, 0)),
        scratch_shapes=[
            pltpu.VMEM((rows, 1), F32),
            pltpu.VMEM((rows, WIDTH_A), F32),
            pltpu.VMEM((rows, WIDTH_A), BF16),
        ],
    )
    return pl.pallas_call(
        functools.partial(_sba_sample_kernel, n_steps=n_steps),
        grid_spec=grid_spec,
        out_shape=jax.ShapeDtypeStruct((nb, t_new, WIDTH_A), F32),
        compiler_params=_params("parallel", "arbitrary"),
        name="sba_sample",
    )(page_table, q, newt, *([cache_t] * PAGES_PER_STEP), u)


def _select_blocks(imp, qpos, n_blocks, n_sel, block_axis):
    sb = lax.broadcasted_iota(I32, imp.shape, block_axis)
    imp = jnp.where(sb * L_SEL > qpos, NEG, imp)
    forced = (sb == 0) | (sb == qpos // L_SEL)
    imp = jnp.where(forced, SEL_BONUS, imp)
    rank = jnp.zeros(imp.shape, F32)
    for i in range(n_blocks):
        vi = imp[i:i + 1, :] if block_axis == 0 else imp[:, i:i + 1]
        beats = jnp.where(vi > imp, 1.0, jnp.where(vi == imp, jnp.where(sb > i, 1.0, 0.0), 0.0))
        rank = rank + beats
    return jnp.where((rank < n_sel) & (sb < n_blocks), 1.0, 0.0)


def _nsa_prompt_kernel(qt_ref, kcmp_ref, vcmpt_ref, kst_ref, vst_ref, kwt_ref, vwt_ref, gt_ref, e_ref, o_ref,
                       *, span):
    g = pl.program_id(1)
    qi = pl.program_id(2)
    t = kst_ref.shape[2]
    ncb = kcmp_ref.shape[2]
    nsb = ncb // CMP_PER_SEL
    q0 = qi * QBLOCK
    tpos_l = q0 + lax.broadcasted_iota(I32, (1, QBLOCK), 1)
    tpos_s = q0 + lax.broadcasted_iota(I32, (QBLOCK, 1), 0)

    r = lax.broadcasted_iota(I32, (ncb, 1), 0)
    n = CMP_PER_SEL * (r % nsb) + r // nsb
    cvalid = ((n + 1) * L_CMP - 1) <= tpos_l
    kcmp = kcmp_ref[0, 0].astype(BF16)
    vcmpt = vcmpt_ref[0, 0].astype(BF16)

    qs = []
    ocs = []
    imp = jnp.zeros((ncb, QBLOCK), F32)
    for m in range(GROUP_B):
        qm = qt_ref[0, m * HEAD_DIM:(m + 1) * HEAD_DIM].astype(F32).T.astype(BF16)
        qs.append(qm)
        zc = jnp.where(cvalid, _nt(kcmp, qm) * SCALE, NEG)
        ex = jnp.exp(zc - jnp.max(zc, axis=0, keepdims=True))
        pc = ex / jnp.sum(ex, axis=0, keepdims=True)
        pc = jnp.where(cvalid, pc, 0.0)
        ocs.append(_nn(vcmpt, pc.astype(BF16)))
        imp = imp + pc
    imp = imp[0:nsb] + imp[nsb:ncb]
    sel = _select_blocks(imp, tpos_l, nsb, min(N_SEL, nsb), 0)
    keymask = _tn(sel.astype(BF16), e_ref[...])

    kpos = lax.broadcasted_iota(I32, (1, t), 1)
    sel_valid = (keymask > 0.5) & (kpos <= tpos_s)
    ws = pl.multiple_of(jnp.maximum(qi - WINDOW // QBLOCK, 0) * QBLOCK, QBLOCK)
    wpos = ws + lax.broadcasted_iota(I32, (1, span), 1)
    win_valid = (wpos <= tpos_s) & (wpos > tpos_s - WINDOW)

    kst = kst_ref[0].astype(BF16)
    vst = vst_ref[0].astype(BF16)
    kwt = kwt_ref[0, :, pl.ds(ws, span)].astype(BF16)
    vwt = vwt_ref[0, :, pl.ds(ws, span)].astype(BF16)
    ones_t = jnp.ones((8, t), BF16)
    ones_w = jnp.ones((8, span), BF16)

    def branch(qm, kt, vt, valid, ones):
        zm = jnp.where(valid, _nn(qm, kt) * SCALE, NEG)
        p = jnp.exp(zm - jnp.max(zm, axis=-1, keepdims=True)).astype(BF16)
        return _nt(vt, p) / _nt(ones, p)[0:1]

    for m in range(GROUP_B):
        os_ = branch(qs[m], kst, vst, sel_valid, ones_t)
        ow = branch(qs[m], kwt, vwt, win_valid, ones_w)
        hb = g * GROUP_B + m
        gc = gt_ref[0, pl.ds(hb, 1), :]
        gs = gt_ref[0, pl.ds(N_HEADS_B + hb, 1), :]
        gw = gt_ref[0, pl.ds(2 * N_HEADS_B + hb, 1), :]
        o_ref[0, m * HEAD_DIM:(m + 1) * HEAD_DIM] = gc * ocs[m] + gs * os_ + gw * ow


def _nsa_prompt(qbt, kcmp, vcmpt, nsat, wint, gt, e):
    nb, _, t = qbt.shape
    nq = t // QBLOCK
    ncb = kcmp.shape[2]
    span = min(WINDOW + QBLOCK, t)
    kv = lambda slot: pl.BlockSpec((1, HEAD_DIM, t), lambda b, g, i: (b, slot * N_KV_B + g, 0))
    return pl.pallas_call(
        functools.partial(_nsa_prompt_kernel, span=span),
        grid=(nb, N_KV_B, nq),
        in_specs=[
            pl.BlockSpec((1, GROUP_B * HEAD_DIM, QBLOCK), lambda b, g, i: (b, g, i)),
            pl.BlockSpec((1, 1, ncb, HEAD_DIM), lambda b, g, i: (b, g, 0, 0)),
            pl.BlockSpec((1, 1, HEAD_DIM, ncb), lambda b, g, i: (b, g, 0, 0)),
            kv(2), kv(3),
            pl.BlockSpec((1, HEAD_DIM, t), lambda b, g, i: (b, g, 0)),
            pl.BlockSpec((1, HEAD_DIM, t), lambda b, g, i: (b, N_KV_B + g, 0)),
            pl.BlockSpec((1, LANES, QBLOCK), lambda b, g, i: (b, 0, i)),
            _const_spec(e.shape),
        ],
        out_specs=pl.BlockSpec((1, GROUP_B * HEAD_DIM, QBLOCK), lambda b, g, i: (b, g, i)),
        out_shape=jax.ShapeDtypeStruct((nb, WIDTH_B, t), F32),
        compiler_params=_params("parallel", "parallel", "arbitrary"),
        name="nsa_prompt",
    )(qbt, kcmp, vcmpt, nsat, nsat, wint, wint, gt, e)


def _nsa_sample_kernel(pt_ref, q_ref, new_ref, win_ref, wnew_ref, g_ref, *rest, n_steps, past):
    page_refs = rest[:PAGES_PER_STEP]
    wexp_ref, e_ref, o_ref, xs_ref, kst_ref, vst_ref, cmp_ref = rest[PAGES_PER_STEP:]
    jj = pl.program_id(1)
    t_new = q_ref.shape[1]
    cw = 2 * KV_WIDTH_B

    for i, ref in enumerate(page_refs):
        off = pl.multiple_of((jj * PAGES_PER_STEP + i) * LANES, LANES)
        rows_t = ref[0, 0:2].reshape(cw, LANES).T
        for half in range(cw // LANES):
            xs_ref[half, pl.ds(off, LANES), :] = rows_t[:, half * LANES:(half + 1) * LANES]
        for g in range(N_KV_B):
            kst_ref[g, :, pl.ds(off, LANES)] = ref[0, 2, g]
            vst_ref[g, :, pl.ds(off, LANES)] = ref[0, 3, g]

    @pl.when(jj == n_steps - 1)
    def _():
        ncb = past // L_CMP
        nsb_past = past // L_SEL
        total = past + LANES
        for g in range(N_KV_B):
            kst_ref[g, :, past:total] = new_ref[0, 2, g]
            vst_ref[g, :, past:total] = new_ref[0, 3, g]

        halves = range(cw // LANES)

        def cbody(l, acc):
            x = jnp.concatenate([xs_ref[hf, pl.ds(l, ncb, stride=L_CMP), :] for hf in halves], axis=1)
            w = wexp_ref[pl.ds(pl.multiple_of(l * cw, cw), cw), :]
            return acc + _nn(x.astype(BF16), w)

        cmp = lax.fori_loop(0, L_CMP, cbody, jnp.zeros((ncb, cw), F32))
        for hf in halves:
            cmp_ref[hf] = cmp[:, hf * LANES:(hf + 1) * LANES]
        cmp_perm = jnp.concatenate(
            [jnp.concatenate([cmp_ref[hf, pl.ds(par, nsb_past, stride=CMP_PER_SEL), :] for hf in halves], axis=1)
             for par in range(CMP_PER_SEL)], axis=0)

        rows = GROUP_B * t_new
        tq = lax.broadcasted_iota(I32, (rows, 1), 0) % t_new
        qpos = past + tq
        qpos_t = past + lax.broadcasted_iota(I32, (t_new, 1), 0)
        r = lax.broadcasted_iota(I32, (1, ncb), 1)
        n = CMP_PER_SEL * (r % nsb_past) + r // nsb_past
        cvalid = ((n + 1) * L_CMP - 1) <= qpos
        kpos = lax.broadcasted_iota(I32, (1, total), 1)
        wlen = win_ref.shape[4]
        wl = lax.broadcasted_iota(I32, (1, wlen + LANES), 1)
        wpos = jnp.where(wl < wlen, past - wlen + wl, past + wl - wlen)
        win_valid = (wpos <= qpos) & (wpos > qpos - WINDOW)
        nsb = nsb_past + 1
        nsb_pad = e_ref.shape[0]
        gates = g_ref[0]

        qall = q_ref[0].astype(F32)
        for g in range(N_KV_B):
            qg = jnp.concatenate(
                [qall[:, (g * GROUP_B + m) * HEAD_DIM:(g * GROUP_B + m + 1) * HEAD_DIM] for m in range(GROUP_B)],
                axis=0).astype(BF16)
            kcmp = cmp_perm[:, g * HEAD_DIM:(g + 1) * HEAD_DIM].astype(BF16)
            vcmp = cmp_perm[:, KV_WIDTH_B + g * HEAD_DIM:KV_WIDTH_B + (g + 1) * HEAD_DIM].astype(BF16)
            zc = jnp.where(cvalid, _nt(qg, kcmp) * SCALE, NEG)
            ex = jnp.exp(zc - jnp.max(zc, axis=-1, keepdims=True))
            pc = ex / jnp.sum(ex, axis=-1, keepdims=True)
            pc = jnp.where(cvalid, pc, 0.0)
            oc = _nn(pc.astype(BF16), vcmp)
            imp = pc[0:t_new]
            for m in range(1, GROUP_B):
                imp = imp + pc[m * t_new:(m + 1) * t_new]
            imp = imp[:, 0:nsb_past] + imp[:, nsb_past:ncb]
            imp = jnp.concatenate([imp, jnp.zeros((t_new, nsb_pad - nsb_past), F32)], axis=1)
            sel = _select_blocks(imp, qpos_t, nsb, min(N_SEL, nsb), 1)
            keymask = _nn(sel.astype(BF16), e_ref[...])
            keymask = jnp.concatenate([keymask] * GROUP_B, axis=0)
            sel_valid = (keymask > 0.5) & (kpos <= qpos)

            def branch(kt, vt, valid):
                zm = jnp.where(valid, _nn(qg, kt) * SCALE, NEG)
                p = jnp.exp(zm - jnp.max(zm, axis=-1, keepdims=True))
                l = jnp.sum(p, axis=-1, keepdims=True)
                return _nt(p.astype(BF16), vt) / l

            os_ = branch(kst_ref[g].astype(BF16), vst_ref[g].astype(BF16), sel_valid)
            kwt = jnp.concatenate([win_ref[0, 0, g], wnew_ref[0, 0, g]], axis=1).astype(BF16)
            vwt = jnp.concatenate([win_ref[0, 1, g], wnew_ref[0, 1, g]], axis=1).astype(BF16)
            ow = branch(kwt, vwt, win_valid)
            for m in range(GROUP_B):
                hb = g * GROUP_B + m
                sl = slice(m * t_new, (m + 1) * t_new)
                o_ref[0, :, hb * HEAD_DIM:(hb + 1) * HEAD_DIM] = (
                    gates[:, hb:hb + 1] * oc[sl]
                    + gates[:, N_HEADS_B + hb:N_HEADS_B + hb + 1] * os_[sl]
                    + gates[:, 2 * N_HEADS_B + hb:2 * N_HEADS_B + hb + 1] * ow[sl])


def _nsa_sample(page_table, q, newt, win_t, wnew_t, gates, cache_t, wexp, e):
    nb, t_new, _ = q.shape
    n_pages = page_table.shape[1]
    n_steps = n_pages // PAGES_PER_STEP
    past = n_pages * LANES
    wlen = win_t.shape[4]
    cw = 2 * KV_WIDTH_B

    def page_spec(i):
        def imap(b, j, pt):
            return (pt[b, j * PAGES_PER_STEP + i], 0, 0, 0, 0)
        return pl.BlockSpec((1, 4, N_KV_B, HEAD_DIM, LANES), imap)

    grid_spec = pltpu.PrefetchScalarGridSpec(
        num_scalar_prefetch=1,
        grid=(nb, n_steps),
        in_specs=[
            pl.BlockSpec((1, t_new, WIDTH_B), lambda b, j, pt: (b, 0, 0)),
            pl.BlockSpec((1, 4, N_KV_B, HEAD_DIM, LANES), lambda b, j, pt: (b, 0, 0, 0, 0)),
            pl.BlockSpec((1, 2, N_KV_B, HEAD_DIM, wlen), lambda b, j, pt: (b, 0, 0, 0, 0)),
            pl.BlockSpec((1, 2, N_KV_B, HEAD_DIM, LANES), lambda b, j, pt: (b, 0, 0, 0, 0)),
            pl.BlockSpec((1, t_new, LANES), lambda b, j, pt: (b, 0, 0)),
            *[page_spec(i) for i in range(PAGES_PER_STEP)],
            _const_spec(wexp.shape),
            _const_spec(e.shape),
        ],
        out_specs=pl.BlockSpec((1, t_new, WIDTH_B), lambda b, j, pt: (b, 0, 0)),
        scratch_shapes=[
            pltpu.VMEM((cw // LANES, past, LANES), F32),
            pltpu.VMEM((N_KV_B, HEAD_DIM, past + LANES), F32),
            pltpu.VMEM((N_KV_B, HEAD_DIM, past + LANES), F32),
            pltpu.VMEM((cw // LANES, past // L_CMP, LANES), F32),
        ],
    )
    return pl.pallas_call(
        functools.partial(_nsa_sample_kernel, n_steps=n_steps, past=past),
        grid_spec=grid_spec,
        out_shape=jax.ShapeDtypeStruct((nb, t_new, WIDTH_B), F32),
        compiler_params=_params("parallel", "arbitrary"),
        name="nsa_sample",
    )(page_table, q, newt, win_t, wnew_t, gates, *([cache_t] * PAGES_PER_STEP), wexp, e)


def _combine_kernel(x_ref, oat_ref, obt_ref, na_ref, nb_ref, wout_ref, nm_ref, wup_ref, wdown_ref, y_ref):
    def norm_t(o, gcol):
        return o * lax.rsqrt(jnp.mean(o * o, axis=0, keepdims=True) + EPS) * gcol

    mixed_t = jnp.concatenate([norm_t(oat_ref[0], na_ref[...]), norm_t(obt_ref[0], nb_ref[...])], axis=0)
    x1 = x_ref[0] + _tn(mixed_t.astype(BF16), wout_ref[...])
    h = x1 * lax.rsqrt(jnp.mean(x1 * x1, axis=-1, keepdims=True) + EPS) * nm_ref[...]
    up = jnp.square(jnp.maximum(_nn(h.astype(BF16), wup_ref[...]), 0.0))
    y_ref[0] = x1 + _nn(up.astype(BF16), wdown_ref[...])


def _combine(x, oat, obt, na_col, nb_col, wout, nm, wup, wdown, tile):
    nb, t, d = x.shape
    return pl.pallas_call(
        _combine_kernel,
        grid=(nb, t // tile),
        in_specs=[
            pl.BlockSpec((1, tile, d), lambda b, i: (b, i, 0)),
            pl.BlockSpec((1, WIDTH_A, tile), lambda b, i: (b, 0, i)),
            pl.BlockSpec((1, WIDTH_B, tile), lambda b, i: (b, 0, i)),
            _const_spec(na_col.shape), _const_spec(nb_col.shape), _const_spec(wout.shape),
            _const_spec(nm.shape), _const_spec(wup.shape), _const_spec(wdown.shape),
        ],
        out_specs=pl.BlockSpec((1, tile, d), lambda b, i: (b, i, 0)),
        out_shape=jax.ShapeDtypeStruct((nb, t, d), F32),
        compiler_params=_params("parallel", "parallel"),
        name="combine_mlp",
    )(x, oat, obt, na_col, nb_col, wout, nm, wup, wdown)


def _expanded_cmp_weights(w_cmp_k, w_cmp_v):
    wk = w_cmp_k.reshape(L_CMP, HEAD_DIM, HEAD_DIM)
    wv = w_cmp_v.reshape(L_CMP, HEAD_DIM, HEAD_DIM)
    slots = [wk] * N_KV_B + [wv] * N_KV_B
    ns = len(slots)
    zero = jnp.zeros_like(wk)
    rows = [jnp.concatenate([slots[i] if i == j else zero for j in range(ns)], axis=2) for i in range(ns)]
    return jnp.concatenate(rows, axis=1).reshape(L_CMP * ns * HEAD_DIM, ns * HEAD_DIM).astype(BF16)


def _expansion(n_rows, n_keys):
    j = np.arange(n_rows)[:, None]
    s = np.arange(n_keys)[None, :]
    return jnp.asarray(s // L_SEL == j, dtype=BF16)


def _kv_out(xt, n_slots, n_heads):
    b, _, t = xt.shape
    return xt.reshape(b, n_slots, n_heads, HEAD_DIM, t).transpose(0, 4, 1, 2, 3)[None]


def kernel(x_prompt, x_sample, cache_sba_kv, cache_nsa_kv, state_win_kv, page_table, norm_attn, w_in, b_gate,
           q_norm_b, k_norm_b, w_cmp_k, w_cmp_v, norm_out_a, norm_out_b, w_out, norm_mlp, w_up, w_down):
    assert norm_attn.shape[0] == 1, "single trunk layer"
    nb, t, d = x_prompt.shape
    db, t_new, _ = x_sample.shape
    n_pages = page_table.shape[1]
    past = n_pages * LANES
    assert cache_sba_kv.shape[2] == LANES and t % QBLOCK == 0 and t_new < L_CMP
    assert n_pages % PAGES_PER_STEP == 0 and t >= WINDOW and state_win_kv.shape[2] == WINDOW

    wt = jnp.pad(jnp.transpose(w_in[0]), ((0, PROJ_ROWS - w_in.shape[2]), (0, 0))).astype(BF16)
    bg_col = jnp.pad(b_gate[0], (0, LANES - N_GATE))[:, None]
    qn_col = q_norm_b[0][:, None]
    kn_cols = k_norm_b[0][:, :, None]
    wexp = _expanded_cmp_weights(w_cmp_k[0], w_cmp_v[0])
    na_col = norm_out_a[0][:, None]
    nb_col = norm_out_b[0][:, None]
    wout = w_out[0].astype(BF16)
    wup = w_up[0].astype(BF16)
    wdown = w_down[0].astype(BF16)
    sidx = np.arange(LANES)
    u = jnp.asarray(sidx[:, None] > sidx[None, :], dtype=BF16)

    tile = min(256, t)
    qat, sbat, qbt, nsat, wint, gt, cmpin = _project(
        x_prompt, jnp.arange(t), norm_attn, wt, bg_col, qn_col, kn_cols, tile)
    ncb = t // L_CMP
    nsb = ncb // CMP_PER_SEL
    cw = 2 * KV_WIDTH_B
    xflat = cmpin.reshape(nb, nsb, CMP_PER_SEL, L_CMP * cw).transpose(0, 2, 1, 3).reshape(nb * ncb, L_CMP * cw)
    cmp = _compress(xflat, wexp).reshape(nb, ncb, 2 * N_KV_B, HEAD_DIM)
    kcmp = cmp[:, :, 0:N_KV_B].transpose(0, 2, 1, 3)
    vcmpt = cmp[:, :, N_KV_B:].transpose(0, 2, 3, 1)
    oat = _sba_prompt(qat, sbat, u)
    obt = _nsa_prompt(qbt, kcmp, vcmpt, nsat, wint, gt, _expansion(nsb, t))
    y_prompt = _combine(x_prompt, oat, obt, na_col, nb_col, wout, norm_mlp, wup, wdown, tile)

    cols = db * t_new
    pos_s = past + jnp.arange(cols) % t_new
    qat_s, sbat_s, qbt_s, nsat_s, wint_s, gt_s, _ = _project(
        x_sample.reshape(1, cols, d), pos_s, norm_attn, wt, bg_col, qn_col, kn_cols, min(256, cols))

    def per_batch_rows(xt):
        return xt[0].reshape(xt.shape[1], db, t_new).transpose(1, 2, 0)

    def per_batch_padded(xt, n_slots, n_heads):
        x5 = xt[0].reshape(n_slots, n_heads, HEAD_DIM, db, t_new).transpose(3, 0, 1, 2, 4)
        return jnp.pad(x5, ((0, 0),) * 4 + ((0, LANES - t_new),))

    sba_cache_t = jnp.transpose(cache_sba_kv[0], (0, 2, 3, 4, 1))
    nsa_cache_t = jnp.transpose(cache_nsa_kv[0], (0, 2, 3, 4, 1))
    win_t = jnp.transpose(state_win_kv[0], (0, 2, 3, 4, 1))

    oa_s = _sba_sample(page_table, per_batch_rows(qat_s), per_batch_padded(sbat_s, 2, N_HEADS_A), sba_cache_t, u)
    nsb_pad = -(-(past // L_SEL + 1) // LANES) * LANES
    ob_s = _nsa_sample(page_table, per_batch_rows(qbt_s), per_batch_padded(nsat_s, 4, N_KV_B), win_t,
                       per_batch_padded(wint_s, 2, N_KV_B), per_batch_rows(gt_s), nsa_cache_t, wexp,
                       _expansion(nsb_pad, past + LANES))
    to_t = lambda o: o.reshape(cols, o.shape[2]).T[None]
    y_sample = _combine(x_sample.reshape(1, cols, d), to_t(oa_s), to_t(ob_s), na_col, nb_col, wout, norm_mlp,
                        wup, wdown, min(256, cols)).reshape(db, t_new, d)

    def sample_kv(xt, n_slots, n_heads):
        return xt[0].reshape(n_slots, n_heads, HEAD_DIM, db, t_new).transpose(3, 4, 0, 1, 2)[None]

    win_new = sample_kv(wint_s, 2, N_KV_B)
    win_s = jnp.concatenate([state_win_kv[:, :, t_new:], win_new], axis=2)
    return (y_prompt, y_sample, _kv_out(sbat, 2, N_HEADS_A), _kv_out(nsat, 4, N_KV_B),
            _kv_out(wint[:, :, t - WINDOW:], 2, N_KV_B), sample_kv(sbat_s, 2, N_HEADS_A),
            sample_kv(nsat_s, 4, N_KV_B), win_s)
```

```python
import functools

import jax
import jax.numpy as jnp
import numpy as np
from jax import lax
from jax.experimental import pallas as pl
from jax.experimental.pallas import tpu as pltpu

F32 = jnp.float32
BF16 = jnp.bfloat16
I32 = jnp.int32

HEAD_DIM = 64
N_HEADS_A = 8
N_HEADS_B = 8
N_KV_B = 2
GROUP_B = N_HEADS_B // N_KV_B
WIDTH_A = N_HEADS_A * HEAD_DIM
WIDTH_B = N_HEADS_B * HEAD_DIM
KV_WIDTH_B = N_KV_B * HEAD_DIM
ROPE_DIM = HEAD_DIM // 4
ROPE_HALF = ROPE_DIM // 2
ROPE_THETA = 500000.0
L_CMP = 32
L_SEL = 64
CMP_PER_SEL = L_SEL // L_CMP
N_SEL = 16
WINDOW = 512
QBLOCK = 128
N_BRANCH = 3
EPS = 1e-6
NEG = -1e30
SEL_BONUS = 1e4
SCALE = HEAD_DIM ** -0.5

LANES = 128
PAGES_PER_STEP = 8
VMEM_LIMIT = 56 * 1024 * 1024

OFF_QA = 0
OFF_SBA = WIDTH_A
OFF_QB = 3 * WIDTH_A
OFF_NSA = OFF_QB + WIDTH_B
OFF_WIN = OFF_NSA + 4 * KV_WIDTH_B
OFF_GATE = OFF_WIN + 2 * KV_WIDTH_B
N_GATE = N_BRANCH * N_HEADS_B
PROJ_ROWS = OFF_GATE + LANES

ATT_SEL = 2 * WIDTH_A
ATT_WIN = ATT_SEL + 2 * KV_WIDTH_B
ATT_ROWS = ATT_WIN + 2 * KV_WIDTH_B


def _nt(a, b):
    return lax.dot_general(a, b, (((1,), (1,)), ((), ())), preferred_element_type=F32)


def _tn(a, b):
    return lax.dot_general(a, b, (((0,), (0,)), ((), ())), preferred_element_type=F32)


def _nn(a, b):
    return jnp.dot(a, b, preferred_element_type=F32)


def _split_dot(x, uu):
    hi = x.astype(BF16)
    lo = (x - hi.astype(F32)).astype(BF16)
    return _nn(jnp.concatenate([hi, lo], axis=1), uu)


def _params(*sem):
    return pltpu.CompilerParams(dimension_semantics=sem, vmem_limit_bytes=VMEM_LIMIT)


def _const_spec(shape):
    nd = len(shape)
    return pl.BlockSpec(shape, lambda *_: (0,) * nd, pipeline_mode=pl.Buffered(1))


def _norm_rope_t(xt, gcol, cos, sin):
    ms = jnp.mean(xt * xt, axis=0, keepdims=True)
    y = xt * lax.rsqrt(ms + EPS) * gcol
    x1 = y[0:ROPE_HALF]
    x2 = y[ROPE_HALF:ROPE_DIM]
    return jnp.concatenate([x1 * cos - x2 * sin, x2 * cos + x1 * sin, y[ROPE_DIM:]], axis=0)


def _proj_kernel(x_ref, g_ref, wt_ref, bg_ref, qn_ref, kn_ref, cos_ref, sin_ref,
                 qat_ref, sbat_ref, qbt_ref, nsat_ref, wint_ref, gt_ref, cmpin_ref, attb_ref):
    x = x_ref[0]
    ms = jnp.mean(x * x, axis=-1, keepdims=True)
    h = (x * lax.rsqrt(ms + EPS)) * g_ref[...]
    pt = _nt(wt_ref[...], h.astype(BF16))
    cos = cos_ref[...]
    sin = sin_ref[...]
    qat_ref[0] = pt[OFF_QA:OFF_QA + WIDTH_A].astype(BF16)
    sbat_ref[0] = pt[OFF_SBA:OFF_SBA + 2 * WIDTH_A]
    attb_ref[0, 0:2 * WIDTH_A] = pt[OFF_SBA:OFF_SBA + 2 * WIDTH_A].astype(BF16)
    qn = qn_ref[...]
    for hd in range(N_HEADS_B):
        lo = OFF_QB + hd * HEAD_DIM
        qbt_ref[0, hd * HEAD_DIM:(hd + 1) * HEAD_DIM] = _norm_rope_t(pt[lo:lo + HEAD_DIM], qn, cos, sin).astype(BF16)
    cmp_rows = []
    for slot in range(4 * N_KV_B):
        lo = OFF_NSA + slot * HEAD_DIM
        blk = pt[lo:lo + HEAD_DIM]
        kind = slot // N_KV_B
        if kind == 0:
            blk = _norm_rope_t(blk, kn_ref[0], cos, sin)
        elif kind == 2:
            blk = _norm_rope_t(blk, kn_ref[1], cos, sin)
        if kind < 2:
            cmp_rows.append(blk)
        else:
            lo = ATT_SEL + (slot - 2 * N_KV_B) * HEAD_DIM
            attb_ref[0, lo:lo + HEAD_DIM] = blk.astype(BF16)
        nsat_ref[0, slot * HEAD_DIM:(slot + 1) * HEAD_DIM] = blk
    for slot in range(2 * N_KV_B):
        lo = OFF_WIN + slot * HEAD_DIM
        blk = pt[lo:lo + HEAD_DIM]
        if slot // N_KV_B == 0:
            blk = _norm_rope_t(blk, kn_ref[2], cos, sin)
        wint_ref[0, slot * HEAD_DIM:(slot + 1) * HEAD_DIM] = blk
        attb_ref[0, ATT_WIN + slot * HEAD_DIM:ATT_WIN + (slot + 1) * HEAD_DIM] = blk.astype(BF16)
    gt_ref[0] = jax.nn.sigmoid(pt[OFF_GATE:OFF_GATE + LANES] + bg_ref[...])
    cmpin_ref[0] = jnp.concatenate(cmp_rows, axis=0).T.astype(BF16)


def _project(x, pos, norm_attn, wt, bg_col, qn_col, kn_cols, tile):
    nb, t, d = x.shape
    half = ROPE_HALF
    inv = ROPE_THETA ** (-jnp.arange(half, dtype=F32) / half)
    ang = inv[:, None] * pos.astype(F32)[None, :]
    cos = jnp.cos(ang)
    sin = jnp.sin(ang)
    grid = (nb, t // tile)
    row = lambda n: pl.BlockSpec((1, n, tile), lambda b, i: (b, 0, i))
    out_shape = (
        jax.ShapeDtypeStruct((nb, WIDTH_A, t), BF16),
        jax.ShapeDtypeStruct((nb, 2 * WIDTH_A, t), F32),
        jax.ShapeDtypeStruct((nb, WIDTH_B, t), BF16),
        jax.ShapeDtypeStruct((nb, 4 * KV_WIDTH_B, t), F32),
        jax.ShapeDtypeStruct((nb, 2 * KV_WIDTH_B, t), F32),
        jax.ShapeDtypeStruct((nb, LANES, t), F32),
        jax.ShapeDtypeStruct((nb, t, 2 * KV_WIDTH_B), BF16),
        jax.ShapeDtypeStruct((nb, ATT_ROWS, t), BF16),
    )
    return pl.pallas_call(
        _proj_kernel,
        grid=grid,
        in_specs=[
            pl.BlockSpec((1, tile, d), lambda b, i: (b, i, 0)),
            _const_spec((1, d)),
            _const_spec((PROJ_ROWS, d)),
            _const_spec((LANES, 1)),
            _const_spec((HEAD_DIM, 1)),
            _const_spec((N_BRANCH, HEAD_DIM, 1)),
            pl.BlockSpec((half, tile), lambda b, i: (0, i)),
            pl.BlockSpec((half, tile), lambda b, i: (0, i)),
        ],
        out_specs=(row(WIDTH_A), row(2 * WIDTH_A), row(WIDTH_B), row(4 * KV_WIDTH_B), row(2 * KV_WIDTH_B),
                   row(LANES), pl.BlockSpec((1, tile, 2 * KV_WIDTH_B), lambda b, i: (b, i, 0)), row(ATT_ROWS)),
        out_shape=out_shape,
        compiler_params=_params("parallel", "parallel"),
        name="project",
    )(x, norm_attn, wt, bg_col, qn_col, kn_cols, cos, sin)


def _compress_kernel(x_ref, w_ref, o_ref):
    o_ref[...] = _nn(x_ref[...], w_ref[...])


def _compress(xflat, wexp):
    rows, k = xflat.shape
    rb = min(rows, 256)
    return pl.pallas_call(
        _compress_kernel,
        grid=(rows // rb,),
        in_specs=[pl.BlockSpec((rb, k), lambda i: (i, 0)), _const_spec(wexp.shape)],
        out_specs=pl.BlockSpec((rb, wexp.shape[1]), lambda i: (i, 0)),
        out_shape=jax.ShapeDtypeStruct((rows, wexp.shape[1]), F32),
        compiler_params=_params("parallel"),
        name="compress",
    )(xflat, wexp)


def _log_sigmoids(z):
    lb = jnp.minimum(z, 0.0) - jnp.log(1.0 + jnp.exp(-jnp.abs(z)))
    return lb, lb - z


def _stick_weights(lb, lr, c, uu):
    rows = lb.shape[0]
    nblk = lb.shape[1] // LANES
    blk = lambda x, k: x[:, k * LANES:(k + 1) * LANES]
    stacked = jnp.concatenate([blk(lr, k) for k in range(nblk)], axis=0)
    au = _split_dot(stacked, uu)
    parts = [None] * nblk
    for k in reversed(range(nblk)):
        auk = au[k * rows:(k + 1) * rows]
        parts[k] = jnp.exp(blk(lb, k) + auk[:, :LANES] + c)
        c = c + auk[:, LANES:]
    return jnp.concatenate(parts, axis=1), c


def _sba_prompt_kernel(q_ref, kv_ref, uu_ref, o_ref, *, keys):
    qi = pl.program_id(1)
    heads = range(N_HEADS_A)
    rows = N_HEADS_A * QBLOCK
    qs = [q_ref[0, h * HEAD_DIM:(h + 1) * HEAD_DIM].astype(F32).T.astype(BF16) for h in heads]
    uu = uu_ref[...]
    tpos = qi * QBLOCK + lax.broadcasted_iota(I32, (rows, keys), 0) % QBLOCK
    kidx = lax.broadcasted_iota(I32, (rows, keys), 1)

    def step(p, carry, masked):
        c, accs = carry
        off = pl.multiple_of(p * keys, keys)
        z = jnp.concatenate(
            [_nn(qs[h], kv_ref[0, h * HEAD_DIM:(h + 1) * HEAD_DIM, pl.ds(off, keys)]) for h in heads], axis=0)
        lb, lr = _log_sigmoids(z * SCALE)
        if masked:
            valid = off + kidx < tpos
            lr = jnp.where(valid, lr, 0.0)
        a, c = _stick_weights(lb, lr, c, uu)
        if masked:
            a = jnp.where(valid, a, 0.0)
        a = a.astype(BF16)
        accs = tuple(
            accs[h] + _nt(kv_ref[0, WIDTH_A + h * HEAD_DIM:WIDTH_A + (h + 1) * HEAD_DIM, pl.ds(off, keys)],
                          a[h * QBLOCK:(h + 1) * QBLOCK]) for h in heads)
        return c, accs

    top = (qi * QBLOCK) // keys
    carry = (jnp.zeros((rows, LANES), F32), tuple(jnp.zeros((HEAD_DIM, QBLOCK), F32) for _ in heads))
    carry = step(top, carry, True)
    carry = lax.fori_loop(0, top, lambda i, cr: step(top - 1 - i, cr, False), carry)
    for h in heads:
        o_ref[0, h * HEAD_DIM:(h + 1) * HEAD_DIM] = carry[1][h]


def _sba_prompt(qat, attb, uu):
    nb, _, t = qat.shape
    nq = t // QBLOCK
    keys = 2 * QBLOCK if nq % 2 == 0 else QBLOCK
    return pl.pallas_call(
        functools.partial(_sba_prompt_kernel, keys=keys),
        grid=(nb, nq),
        in_specs=[
            pl.BlockSpec((1, WIDTH_A, QBLOCK), lambda b, i: (b, 0, i)),
            pl.BlockSpec((1, 2 * WIDTH_A, t), lambda b, i: (b, 0, 0)),
            _const_spec(uu.shape),
        ],
        out_specs=pl.BlockSpec((1, WIDTH_A, QBLOCK), lambda b, i: (b, 0, i)),
        out_shape=jax.ShapeDtypeStruct((nb, WIDTH_A, t), F32),
        compiler_params=_params("parallel", "arbitrary"),
        name="sba_prompt",
    )(qat, attb, uu)


def _sba_sample_kernel(pt_ref, q_ref, new_ref, *rest, n_steps):
    page_refs = rest[:PAGES_PER_STEP]
    uu_ref, o_ref, c_ref, acc_ref, qbd_ref = rest[PAGES_PER_STEP:]
    jj = pl.program_id(1)
    t_new = q_ref.shape[1]
    rows = N_HEADS_A * t_new

    def process(refs, valid):
        kt = jnp.concatenate([r[0, 0].reshape(WIDTH_A, LANES) for r in refs], axis=1).astype(BF16)
        vt = jnp.concatenate([r[0, 1].reshape(WIDTH_A, LANES) for r in refs], axis=1).astype(BF16)
        lb, lr = _log_sigmoids(_nn(qbd_ref[...], kt) * SCALE)
        if valid is not None:
            lr = jnp.where(valid, lr, 0.0)
        a, c = _stick_weights(lb, lr, c_ref[...], uu_ref[...])
        c_ref[...] = c
        if valid is not None:
            a = jnp.where(valid, a, 0.0)
        acc_ref[...] += _nt(a.astype(BF16), vt)

    @pl.when(jj == 0)
    def _():
        q = jnp.concatenate([q_ref[0].astype(F32)] * N_HEADS_A, axis=0)
        r = lax.broadcasted_iota(I32, q.shape, 0) // t_new
        cidx = lax.broadcasted_iota(I32, q.shape, 1) // HEAD_DIM
        qbd_ref[...] = jnp.where(r == cidx, q, 0.0).astype(BF16)
        c_ref[...] = jnp.zeros_like(c_ref)
        acc_ref[...] = jnp.zeros_like(acc_ref)
        tq = lax.broadcasted_iota(I32, (rows, LANES), 0) % t_new
        s = lax.broadcasted_iota(I32, (rows, LANES), 1)
        process([new_ref], s < tq)

    process(page_refs, None)

    @pl.when(jj == n_steps - 1)
    def _():
        acc = acc_ref[...]
        r = lax.broadcasted_iota(I32, acc.shape, 0) // t_new
        cidx = lax.broadcasted_iota(I32, acc.shape, 1) // HEAD_DIM
        acc = jnp.where(r == cidx, acc, 0.0)
        out = acc[0:t_new]
        for hd in range(1, N_HEADS_A):
            out = out + acc[hd * t_new:(hd + 1) * t_new]
        o_ref[0] = out


def _sba_sample(page_table, q, newt, cache_t, uu):
    nb, t_new, _ = q.shape
    n_pages = page_table.shape[1]
    n_steps = n_pages // PAGES_PER_STEP
    rows = N_HEADS_A * t_new

    def page_spec(i):
        def imap(b, j, pt):
            return (pt[b, n_pages - (j + 1) * PAGES_PER_STEP + i], 0, 0, 0, 0)
        return pl.BlockSpec((1, 2, N_HEADS_A, HEAD_DIM, LANES), imap)

    grid_spec = pltpu.PrefetchScalarGridSpec(
        num_scalar_prefetch=1,
        grid=(nb, n_steps),
        in_specs=[
            pl.BlockSpec((1, t_new, WIDTH_A), lambda b, j, pt: (b, 0, 0)),
            pl.BlockSpec((1, 2, N_HEADS_A, HEAD_DIM, LANES), lambda b, j, pt: (b, 0, 0, 0, 0)),
            *[page_spec(i) for i in range(PAGES_PER_STEP)],
            _const_spec(uu.shape),
        ],
        out_specs=pl.BlockSpec((1, t_new, WIDTH_A), lambda b, j, pt: (b, 0, 0)),
        scratch_shapes=[
            pltpu.VMEM((rows, LANES), F32),
            pltpu.VMEM((rows, WIDTH_A), F32),
            pltpu.VMEM((rows, WIDTH_A), BF16),
        ],
    )
    return pl.pallas_call(
        functools.partial(_sba_sample_kernel, n_steps=n_steps),
        grid_spec=grid_spec,
        out_shape=jax.ShapeDtypeStruct((nb, t_new, WIDTH_A), F32),
        compiler_params=_params("parallel", "arbitrary"),
        name="sba_sample",
    )(page_table, q, newt, *([cache_t] * PAGES_PER_STEP), uu)


def _select_blocks(imp, qpos, n_blocks, n_sel, block_axis):
    sb = lax.broadcasted_iota(I32, imp.shape, block_axis)
    imp = jnp.where(sb * L_SEL > qpos, NEG, imp)
    forced = (sb == 0) | (sb == qpos // L_SEL)
    imp = jnp.where(forced, SEL_BONUS, imp)
    rank = jnp.zeros(imp.shape, F32)
    for i in range(n_blocks):
        vi = imp[i:i + 1, :] if block_axis == 0 else imp[:, i:i + 1]
        beats = jnp.where(vi > imp, 1.0, jnp.where(vi == imp, jnp.where(sb > i, 1.0, 0.0), 0.0))
        rank = rank + beats
    return jnp.where((rank < n_sel) & (sb < n_blocks), 1.0, 0.0)


def _nsa_prompt_kernel(qt_ref, kcmp_ref, vcmpt_ref, kst_ref, vst_ref, kwt_ref, vwt_ref, gt_ref, e_ref, o_ref,
                       os_ref, *, span, n_var):
    g = pl.program_id(1)
    qi = pl.program_id(2)
    t = kst_ref.shape[2]
    var_keys = t // n_var
    ncb = kcmp_ref.shape[2]
    nsb = ncb // CMP_PER_SEL
    q0 = qi * QBLOCK
    tpos_l = q0 + lax.broadcasted_iota(I32, (1, QBLOCK), 1)
    tpos_s = q0 + lax.broadcasted_iota(I32, (QBLOCK, 1), 0)

    r = lax.broadcasted_iota(I32, (ncb, 1), 0)
    n = CMP_PER_SEL * (r % nsb) + r // nsb
    cvalid = ((n + 1) * L_CMP - 1) <= tpos_l
    kcmp = kcmp_ref[0, 0].astype(BF16)
    vcmpt = vcmpt_ref[0, 0].astype(BF16)

    qs = []
    ocs = []
    imp = jnp.zeros((ncb, QBLOCK), F32)
    for m in range(GROUP_B):
        qm = qt_ref[0, m * HEAD_DIM:(m + 1) * HEAD_DIM].astype(F32).T.astype(BF16)
        qs.append(qm)
        zc = jnp.where(cvalid, _nt(kcmp, qm) * SCALE, NEG)
        ex = jnp.exp(zc - jnp.max(zc, axis=0, keepdims=True))
        pc = ex / jnp.sum(ex, axis=0, keepdims=True)
        pc = jnp.where(cvalid, pc, 0.0)
        ocs.append(_nn(vcmpt, pc.astype(BF16)))
        imp = imp + pc
    imp = imp[0:nsb] + imp[nsb:ncb]
    sel = _select_blocks(imp, tpos_l, nsb, min(N_SEL, nsb), 0).astype(BF16)

    def branch(qm, kt, vt, valid):
        zm = jnp.where(valid, _nn(qm, kt) * SCALE, NEG)
        p = jnp.exp(zm - jnp.max(zm, axis=-1, keepdims=True)).astype(BF16)
        ones = jnp.ones((8, p.shape[1]), BF16)
        return _nt(vt, p) / _nt(ones, p)[0:1]

    for v in range(n_var):
        nk = (v + 1) * var_keys

        @pl.when(q0 // var_keys == v)
        def _():
            keymask = _tn(sel, e_ref[:, 0:nk])
            kpos = lax.broadcasted_iota(I32, (1, nk), 1)
            valid = (keymask > 0.5) & (kpos <= tpos_s)
            kst = kst_ref[0, :, 0:nk]
            vst = vst_ref[0, :, 0:nk]
            for m in range(GROUP_B):
                os_ref[m * HEAD_DIM:(m + 1) * HEAD_DIM] = branch(qs[m], kst, vst, valid)

    ws = pl.multiple_of(jnp.maximum(qi - WINDOW // QBLOCK, 0) * QBLOCK, QBLOCK)
    wpos = ws + lax.broadcasted_iota(I32, (1, span), 1)
    win_valid = (wpos <= tpos_s) & (wpos > tpos_s - WINDOW)
    kwt = kwt_ref[0, :, pl.ds(ws, span)]
    vwt = vwt_ref[0, :, pl.ds(ws, span)]
    for m in range(GROUP_B):
        ow = branch(qs[m], kwt, vwt, win_valid)
        hb = g * GROUP_B + m
        gc = gt_ref[0, pl.ds(hb, 1), :]
        gs = gt_ref[0, pl.ds(N_HEADS_B + hb, 1), :]
        gw = gt_ref[0, pl.ds(2 * N_HEADS_B + hb, 1), :]
        rows = slice(m * HEAD_DIM, (m + 1) * HEAD_DIM)
        o_ref[0, rows] = gc * ocs[m] + gs * os_ref[rows] + gw * ow


def _nsa_prompt(qbt, kcmp, vcmpt, attb, gt, e):
    nb, _, t = qbt.shape
    nq = t // QBLOCK
    ncb = kcmp.shape[2]
    span = min(WINDOW + QBLOCK, t)
    n_var = 4 if nq % 4 == 0 else 1
    kv = lambda base, slot: pl.BlockSpec(
        (1, HEAD_DIM, t), lambda b, g, i: (b, base // HEAD_DIM + slot * N_KV_B + g, 0))
    return pl.pallas_call(
        functools.partial(_nsa_prompt_kernel, span=span, n_var=n_var),
        grid=(nb, N_KV_B, nq),
        in_specs=[
            pl.BlockSpec((1, GROUP_B * HEAD_DIM, QBLOCK), lambda b, g, i: (b, g, i)),
            pl.BlockSpec((1, 1, ncb, HEAD_DIM), lambda b, g, i: (b, g, 0, 0)),
            pl.BlockSpec((1, 1, HEAD_DIM, ncb), lambda b, g, i: (b, g, 0, 0)),
            kv(ATT_SEL, 0), kv(ATT_SEL, 1), kv(ATT_WIN, 0), kv(ATT_WIN, 1),
            pl.BlockSpec((1, LANES, QBLOCK), lambda b, g, i: (b, 0, i)),
            _const_spec(e.shape),
        ],
        out_specs=pl.BlockSpec((1, GROUP_B * HEAD_DIM, QBLOCK), lambda b, g, i: (b, g, i)),
        out_shape=jax.ShapeDtypeStruct((nb, WIDTH_B, t), F32),
        scratch_shapes=[pltpu.VMEM((GROUP_B * HEAD_DIM, QBLOCK), F32)],
        compiler_params=_params("parallel", "parallel", "arbitrary"),
        name="nsa_prompt",
    )(qbt, kcmp, vcmpt, attb, attb, attb, attb, gt, e)


def _nsa_sample_kernel(pt_ref, q_ref, new_ref, win_ref, wnew_ref, g_ref, *rest, n_steps, past):
    page_refs = rest[:PAGES_PER_STEP]
    wexp_ref, e_ref, perm_ref, o_ref, xs_ref, kst_ref, vst_ref, cmp_ref = rest[PAGES_PER_STEP:]
    jj = pl.program_id(1)
    t_new = q_ref.shape[1]
    cw = 2 * KV_WIDTH_B
    pair_rows = 2 * LANES // L_CMP

    for pr in range(PAGES_PER_STEP // 2):
        pages = jnp.concatenate([page_refs[2 * pr + i][0, 0:2].reshape(cw, LANES) for i in range(2)], axis=1)
        y = _nt(perm_ref[...], pages.astype(BF16))
        row0 = pl.multiple_of((jj * (PAGES_PER_STEP // 2) + pr) * pair_rows, pair_rows)
        for l in range(L_CMP):
            xs_ref[l, pl.ds(row0, pair_rows), :] = y[l * pair_rows:(l + 1) * pair_rows]
    for i, ref in enumerate(page_refs):
        off = pl.multiple_of((jj * PAGES_PER_STEP + i) * LANES, LANES)
        for g in range(N_KV_B):
            kst_ref[g, :, pl.ds(off, LANES)] = ref[0, 2, g].astype(BF16)
            vst_ref[g, :, pl.ds(off, LANES)] = ref[0, 3, g].astype(BF16)

    @pl.when(jj == n_steps - 1)
    def _():
        ncb = past // L_CMP
        nsb_past = past // L_SEL
        total = past + LANES
        for g in range(N_KV_B):
            kst_ref[g, :, past:total] = new_ref[0, 2, g].astype(BF16)
            vst_ref[g, :, past:total] = new_ref[0, 3, g].astype(BF16)

        halves = range(cw // LANES)

        def cbody(l, acc):
            w = wexp_ref[pl.ds(pl.multiple_of(l * cw, cw), cw), :]
            return acc + _nn(xs_ref[l].astype(BF16), w)

        cmp = lax.fori_loop(0, L_CMP, cbody, jnp.zeros((ncb, cw), F32))
        for hf in halves:
            cmp_ref[hf] = cmp[:, hf * LANES:(hf + 1) * LANES]
        cmp_perm = jnp.concatenate(
            [jnp.concatenate([cmp_ref[hf, pl.ds(par, nsb_past, stride=CMP_PER_SEL), :] for hf in halves], axis=1)
             for par in range(CMP_PER_SEL)], axis=0)

        rows = GROUP_B * t_new
        tq = lax.broadcasted_iota(I32, (rows, 1), 0) % t_new
        qpos = past + tq
        qpos_t = past + lax.broadcasted_iota(I32, (t_new, 1), 0)
        r = lax.broadcasted_iota(I32, (1, ncb), 1)
        n = CMP_PER_SEL * (r % nsb_past) + r // nsb_past
        cvalid = ((n + 1) * L_CMP - 1) <= qpos
        kpos = lax.broadcasted_iota(I32, (1, total), 1)
        wlen = win_ref.shape[4]
        wl = lax.broadcasted_iota(I32, (1, wlen + LANES), 1)
        wpos = jnp.where(wl < wlen, past - wlen + wl, past + wl - wlen)
        win_valid = (wpos <= qpos) & (wpos > qpos - WINDOW)
        nsb = nsb_past + 1
        nsb_pad = e_ref.shape[0]
        gates = g_ref[0]

        qall = q_ref[0].astype(F32)
        for g in range(N_KV_B):
            qg = jnp.concatenate(
                [qall[:, (g * GROUP_B + m) * HEAD_DIM:(g * GROUP_B + m + 1) * HEAD_DIM] for m in range(GROUP_B)],
                axis=0).astype(BF16)
            kcmp = cmp_perm[:, g * HEAD_DIM:(g + 1) * HEAD_DIM].astype(BF16)
            vcmp = cmp_perm[:, KV_WIDTH_B + g * HEAD_DIM:KV_WIDTH_B + (g + 1) * HEAD_DIM].astype(BF16)
            zc = jnp.where(cvalid, _nt(qg, kcmp) * SCALE, NEG)
            ex = jnp.exp(zc - jnp.max(zc, axis=-1, keepdims=True))
            pc = ex / jnp.sum(ex, axis=-1, keepdims=True)
            pc = jnp.where(cvalid, pc, 0.0)
            oc = _nn(pc.astype(BF16), vcmp)
            imp = pc[0:t_new]
            for m in range(1, GROUP_B):
                imp = imp + pc[m * t_new:(m + 1) * t_new]
            imp = imp[:, 0:nsb_past] + imp[:, nsb_past:ncb]
            imp = jnp.concatenate([imp, jnp.zeros((t_new, nsb_pad - nsb_past), F32)], axis=1)
            sel = _select_blocks(imp, qpos_t, nsb, min(N_SEL, nsb), 1)
            keymask = _nn(sel.astype(BF16), e_ref[...])
            keymask = jnp.concatenate([keymask] * GROUP_B, axis=0)
            sel_valid = (keymask > 0.5) & (kpos <= qpos)

            def branch(kt, vt, valid):
                zm = jnp.where(valid, _nn(qg, kt) * SCALE, NEG)
                p = jnp.exp(zm - jnp.max(zm, axis=-1, keepdims=True))
                l = jnp.sum(p, axis=-1, keepdims=True)
                return _nt(p.astype(BF16), vt) / l

            os_ = branch(kst_ref[g], vst_ref[g], sel_valid)
            kwt = jnp.concatenate([win_ref[0, 0, g], wnew_ref[0, 0, g]], axis=1).astype(BF16)
            vwt = jnp.concatenate([win_ref[0, 1, g], wnew_ref[0, 1, g]], axis=1).astype(BF16)
            ow = branch(kwt, vwt, win_valid)
            for m in range(GROUP_B):
                hb = g * GROUP_B + m
                sl = slice(m * t_new, (m + 1) * t_new)
                o_ref[0, :, hb * HEAD_DIM:(hb + 1) * HEAD_DIM] = (
                    gates[:, hb:hb + 1] * oc[sl]
                    + gates[:, N_HEADS_B + hb:N_HEADS_B + hb + 1] * os_[sl]
                    + gates[:, 2 * N_HEADS_B + hb:2 * N_HEADS_B + hb + 1] * ow[sl])


def _pair_permutation():
    blocks = LANES // L_CMP
    l, page, n = np.meshgrid(np.arange(L_CMP), np.arange(2), np.arange(blocks), indexing="ij")
    perm = np.zeros((2 * LANES, 2 * LANES), np.float32)
    perm[(l * 2 * blocks + page * blocks + n).ravel(), (page * LANES + n * L_CMP + l).ravel()] = 1.0
    return jnp.asarray(perm, dtype=BF16)


def _nsa_sample(page_table, q, newt, win_t, wnew_t, gates, cache_t, wexp, e):
    nb, t_new, _ = q.shape
    perm = _pair_permutation()
    n_pages = page_table.shape[1]
    n_steps = n_pages // PAGES_PER_STEP
    past = n_pages * LANES
    wlen = win_t.shape[4]
    cw = 2 * KV_WIDTH_B

    def page_spec(i):
        def imap(b, j, pt):
            return (pt[b, j * PAGES_PER_STEP + i], 0, 0, 0, 0)
        return pl.BlockSpec((1, 4, N_KV_B, HEAD_DIM, LANES), imap)

    grid_spec = pltpu.PrefetchScalarGridSpec(
        num_scalar_prefetch=1,
        grid=(nb, n_steps),
        in_specs=[
            pl.BlockSpec((1, t_new, WIDTH_B), lambda b, j, pt: (b, 0, 0)),
            pl.BlockSpec((1, 4, N_KV_B, HEAD_DIM, LANES), lambda b, j, pt: (b, 0, 0, 0, 0)),
            pl.BlockSpec((1, 2, N_KV_B, HEAD_DIM, wlen), lambda b, j, pt: (b, 0, 0, 0, 0)),
            pl.BlockSpec((1, 2, N_KV_B, HEAD_DIM, LANES), lambda b, j, pt: (b, 0, 0, 0, 0)),
            pl.BlockSpec((1, t_new, LANES), lambda b, j, pt: (b, 0, 0)),
            *[page_spec(i) for i in range(PAGES_PER_STEP)],
            _const_spec(wexp.shape),
            _const_spec(e.shape),
            _const_spec(perm.shape),
        ],
        out_specs=pl.BlockSpec((1, t_new, WIDTH_B), lambda b, j, pt: (b, 0, 0)),
        scratch_shapes=[
            pltpu.VMEM((L_CMP, past // L_CMP, cw), F32),
            pltpu.VMEM((N_KV_B, HEAD_DIM, past + LANES), BF16),
            pltpu.VMEM((N_KV_B, HEAD_DIM, past + LANES), BF16),
            pltpu.VMEM((cw // LANES, past // L_CMP, LANES), F32),
        ],
    )
    return pl.pallas_call(
        functools.partial(_nsa_sample_kernel, n_steps=n_steps, past=past),
        grid_spec=grid_spec,
        out_shape=jax.ShapeDtypeStruct((nb, t_new, WIDTH_B), F32),
        compiler_params=_params("parallel", "arbitrary"),
        name="nsa_sample",
    )(page_table, q, newt, win_t, wnew_t, gates, *([cache_t] * PAGES_PER_STEP), wexp, e, perm)


def _combine_kernel(x_ref, oat_ref, obt_ref, na_ref, nb_ref, wout_ref, nm_ref, wup_ref, wdown_ref, y_ref):
    def norm_t(o, gcol):
        return o * lax.rsqrt(jnp.mean(o * o, axis=0, keepdims=True) + EPS) * gcol

    mixed_t = jnp.concatenate([norm_t(oat_ref[0], na_ref[...]), norm_t(obt_ref[0], nb_ref[...])], axis=0)
    x1 = x_ref[0] + _tn(mixed_t.astype(BF16), wout_ref[...])
    h = x1 * lax.rsqrt(jnp.mean(x1 * x1, axis=-1, keepdims=True) + EPS) * nm_ref[...]
    up = jnp.square(jnp.maximum(_nn(h.astype(BF16), wup_ref[...]), 0.0))
    y_ref[0] = x1 + _nn(up.astype(BF16), wdown_ref[...])


def _combine(x, oat, obt, na_col, nb_col, wout, nm, wup, wdown, tile):
    nb, t, d = x.shape
    return pl.pallas_call(
        _combine_kernel,
        grid=(nb, t // tile),
        in_specs=[
            pl.BlockSpec((1, tile, d), lambda b, i: (b, i, 0)),
            pl.BlockSpec((1, WIDTH_A, tile), lambda b, i: (b, 0, i)),
            pl.BlockSpec((1, WIDTH_B, tile), lambda b, i: (b, 0, i)),
            _const_spec(na_col.shape), _const_spec(nb_col.shape), _const_spec(wout.shape),
            _const_spec(nm.shape), _const_spec(wup.shape), _const_spec(wdown.shape),
        ],
        out_specs=pl.BlockSpec((1, tile, d), lambda b, i: (b, i, 0)),
        out_shape=jax.ShapeDtypeStruct((nb, t, d), F32),
        compiler_params=_params("parallel", "parallel"),
        name="combine_mlp",
    )(x, oat, obt, na_col, nb_col, wout, nm, wup, wdown)


def _expanded_cmp_weights(w_cmp_k, w_cmp_v):
    wk = w_cmp_k.reshape(L_CMP, HEAD_DIM, HEAD_DIM)
    wv = w_cmp_v.reshape(L_CMP, HEAD_DIM, HEAD_DIM)
    slots = [wk] * N_KV_B + [wv] * N_KV_B
    ns = len(slots)
    zero = jnp.zeros_like(wk)
    rows = [jnp.concatenate([slots[i] if i == j else zero for j in range(ns)], axis=2) for i in range(ns)]
    return jnp.concatenate(rows, axis=1).reshape(L_CMP * ns * HEAD_DIM, ns * HEAD_DIM).astype(BF16)


def _expansion(n_rows, n_keys):
    j = np.arange(n_rows)[:, None]
    s = np.arange(n_keys)[None, :]
    return jnp.asarray(s // L_SEL == j, dtype=BF16)


def _kv_out(xt, n_slots, n_heads):
    b, _, t = xt.shape
    return xt.reshape(b, n_slots, n_heads, HEAD_DIM, t).transpose(0, 4, 1, 2, 3)[None]


def kernel(x_prompt, x_sample, cache_sba_kv, cache_nsa_kv, state_win_kv, page_table, norm_attn, w_in, b_gate,
           q_norm_b, k_norm_b, w_cmp_k, w_cmp_v, norm_out_a, norm_out_b, w_out, norm_mlp, w_up, w_down):
    assert norm_attn.shape[0] == 1, "single trunk layer"
    nb, t, d = x_prompt.shape
    db, t_new, _ = x_sample.shape
    n_pages = page_table.shape[1]
    past = n_pages * LANES
    assert cache_sba_kv.shape[2] == LANES and t % QBLOCK == 0 and t_new < L_CMP
    assert n_pages % PAGES_PER_STEP == 0 and t >= WINDOW and state_win_kv.shape[2] == WINDOW

    wt = jnp.pad(jnp.transpose(w_in[0]), ((0, PROJ_ROWS - w_in.shape[2]), (0, 0))).astype(BF16)
    bg_col = jnp.pad(b_gate[0], (0, LANES - N_GATE))[:, None]
    qn_col = q_norm_b[0][:, None]
    kn_cols = k_norm_b[0][:, :, None]
    wexp = _expanded_cmp_weights(w_cmp_k[0], w_cmp_v[0])
    na_col = norm_out_a[0][:, None]
    nb_col = norm_out_b[0][:, None]
    wout = w_out[0].astype(BF16)
    wup = w_up[0].astype(BF16)
    wdown = w_down[0].astype(BF16)
    sidx = np.arange(LANES)
    u2 = np.concatenate([sidx[:, None] > sidx[None, :], np.ones((LANES, LANES), bool)], axis=1)
    uu = jnp.asarray(np.concatenate([u2, u2], axis=0), dtype=BF16)

    tile = min(256, t)
    qat, sbat, qbt, nsat, wint, gt, cmpin, attb = _project(
        x_prompt, jnp.arange(t), norm_attn, wt, bg_col, qn_col, kn_cols, tile)
    ncb = t // L_CMP
    nsb = ncb // CMP_PER_SEL
    cw = 2 * KV_WIDTH_B
    xflat = cmpin.reshape(nb, nsb, CMP_PER_SEL, L_CMP * cw).transpose(0, 2, 1, 3).reshape(nb * ncb, L_CMP * cw)
    cmp = _compress(xflat, wexp).reshape(nb, ncb, 2 * N_KV_B, HEAD_DIM)
    kcmp = cmp[:, :, 0:N_KV_B].transpose(0, 2, 1, 3)
    vcmpt = cmp[:, :, N_KV_B:].transpose(0, 2, 3, 1)
    oat = _sba_prompt(qat, attb, uu)
    obt = _nsa_prompt(qbt, kcmp, vcmpt, attb, gt, _expansion(nsb, t))
    y_prompt = _combine(x_prompt, oat, obt, na_col, nb_col, wout, norm_mlp, wup, wdown, tile)

    cols = db * t_new
    pos_s = past + jnp.arange(cols) % t_new
    qat_s, sbat_s, qbt_s, nsat_s, wint_s, gt_s, _, _ = _project(
        x_sample.reshape(1, cols, d), pos_s, norm_attn, wt, bg_col, qn_col, kn_cols, min(256, cols))

    def per_batch_rows(xt):
        return xt[0].reshape(xt.shape[1], db, t_new).transpose(1, 2, 0)

    def per_batch_padded(xt, n_slots, n_heads):
        x5 = xt[0].reshape(n_slots, n_heads, HEAD_DIM, db, t_new).transpose(3, 0, 1, 2, 4)
        return jnp.pad(x5, ((0, 0),) * 4 + ((0, LANES - t_new),))

    sba_cache_t = jnp.transpose(cache_sba_kv[0], (0, 2, 3, 4, 1))
    nsa_cache_t = jnp.transpose(cache_nsa_kv[0], (0, 2, 3, 4, 1))
    win_t = jnp.transpose(state_win_kv[0], (0, 2, 3, 4, 1))

    oa_s = _sba_sample(page_table, per_batch_rows(qat_s), per_batch_padded(sbat_s, 2, N_HEADS_A), sba_cache_t, uu)
    nsb_pad = -(-(past // L_SEL + 1) // LANES) * LANES
    ob_s = _nsa_sample(page_table, per_batch_rows(qbt_s), per_batch_padded(nsat_s, 4, N_KV_B), win_t,
                       per_batch_padded(wint_s, 2, N_KV_B), per_batch_rows(gt_s), nsa_cache_t, wexp,
                       _expansion(nsb_pad, past + LANES))
    to_t = lambda o: o.reshape(cols, o.shape[2]).T[None]
    y_sample = _combine(x_sample.reshape(1, cols, d), to_t(oa_s), to_t(ob_s), na_col, nb_col, wout, norm_mlp,
                        wup, wdown, min(256, cols)).reshape(db, t_new, d)

    def sample_kv(xt, n_slots, n_heads):
        return xt[0].reshape(n_slots, n_heads, HEAD_DIM, db, t_new).transpose(3, 4, 0, 1, 2)[None]

    win_new = sample_kv(wint_s, 2, N_KV_B)
    win_s = jnp.concatenate([state_win_kv[:, :, t_new:], win_new], axis=2)
    return (y_prompt, y_sample, _kv_out(sbat, 2, N_HEADS_A), _kv_out(nsat, 4, N_KV_B),
            _kv_out(wint[:, :, t - WINDOW:], 2, N_KV_B), sample_kv(sbat_s, 2, N_HEADS_A),
            sample_kv(nsat_s, 4, N_KV_B), win_s)
```

```python
import functools

import jax
import jax.numpy as jnp
import numpy as np
from jax import lax
from jax.experimental import pallas as pl
from jax.experimental.pallas import tpu as pltpu

F32 = jnp.float32
BF16 = jnp.bfloat16
I32 = jnp.int32

HEAD_DIM = 64
N_HEADS_A = 8
N_HEADS_B = 8
N_KV_B = 2
GROUP_B = N_HEADS_B // N_KV_B
WIDTH_A = N_HEADS_A * HEAD_DIM
WIDTH_B = N_HEADS_B * HEAD_DIM
KV_WIDTH_B = N_KV_B * HEAD_DIM
ROPE_DIM = HEAD_DIM // 4
ROPE_HALF = ROPE_DIM // 2
ROPE_THETA = 500000.0
L_CMP = 32
L_SEL = 64
CMP_PER_SEL = L_SEL // L_CMP
N_SEL = 16
WINDOW = 512
QBLOCK = 128
N_BRANCH = 3
EPS = 1e-6
NEG = -1e30
SEL_BONUS = 1e4
SCALE = HEAD_DIM ** -0.5

LANES = 128
PAGES_PER_STEP = 8
VMEM_LIMIT = 56 * 1024 * 1024

OFF_QA = 0
OFF_SBA = WIDTH_A
OFF_QB = 3 * WIDTH_A
OFF_NSA = OFF_QB + WIDTH_B
OFF_WIN = OFF_NSA + 4 * KV_WIDTH_B
OFF_GATE = OFF_WIN + 2 * KV_WIDTH_B
N_GATE = N_BRANCH * N_HEADS_B
PROJ_ROWS = OFF_GATE + LANES

ATT_SEL = 2 * WIDTH_A
ATT_WIN = ATT_SEL + 2 * KV_WIDTH_B
ATT_ROWS = ATT_WIN + 2 * KV_WIDTH_B


def _nt(a, b):
    return lax.dot_general(a, b, (((1,), (1,)), ((), ())), preferred_element_type=F32)


def _tn(a, b):
    return lax.dot_general(a, b, (((0,), (0,)), ((), ())), preferred_element_type=F32)


def _nn(a, b):
    return jnp.dot(a, b, preferred_element_type=F32)


def _split_dot(x, uu):
    hi = x.astype(BF16)
    lo = (x - hi.astype(F32)).astype(BF16)
    return _nn(jnp.concatenate([hi, lo], axis=1), uu)


def _params(*sem):
    return pltpu.CompilerParams(dimension_semantics=sem, vmem_limit_bytes=VMEM_LIMIT)


def _const_spec(shape):
    nd = len(shape)
    return pl.BlockSpec(shape, lambda *_: (0,) * nd, pipeline_mode=pl.Buffered(1))


def _norm_rope_t(xt, gcol, cos, sin):
    ms = jnp.mean(xt * xt, axis=0, keepdims=True)
    y = xt * lax.rsqrt(ms + EPS) * gcol
    x1 = y[0:ROPE_HALF]
    x2 = y[ROPE_HALF:ROPE_DIM]
    return jnp.concatenate([x1 * cos - x2 * sin, x2 * cos + x1 * sin, y[ROPE_DIM:]], axis=0)


def _proj_kernel(x_ref, g_ref, wt_ref, bg_ref, qn_ref, kn_ref, cos_ref, sin_ref,
                 qat_ref, sbat_ref, qbt_ref, nsat_ref, wint_ref, gt_ref, cmpin_ref, attb_ref):
    x = x_ref[0]
    ms = jnp.mean(x * x, axis=-1, keepdims=True)
    h = (x * lax.rsqrt(ms + EPS)) * g_ref[...]
    pt = _nt(wt_ref[...], h.astype(BF16))
    cos = cos_ref[...]
    sin = sin_ref[...]
    qat_ref[0] = pt[OFF_QA:OFF_QA + WIDTH_A].astype(BF16)
    sbat_ref[0] = pt[OFF_SBA:OFF_SBA + 2 * WIDTH_A]
    attb_ref[0, 0:2 * WIDTH_A] = pt[OFF_SBA:OFF_SBA + 2 * WIDTH_A].astype(BF16)
    qn = qn_ref[...]
    for hd in range(N_HEADS_B):
        lo = OFF_QB + hd * HEAD_DIM
        qbt_ref[0, hd * HEAD_DIM:(hd + 1) * HEAD_DIM] = _norm_rope_t(pt[lo:lo + HEAD_DIM], qn, cos, sin).astype(BF16)
    cmp_rows = []
    for slot in range(4 * N_KV_B):
        lo = OFF_NSA + slot * HEAD_DIM
        blk = pt[lo:lo + HEAD_DIM]
        kind = slot // N_KV_B
        if kind == 0:
            blk = _norm_rope_t(blk, kn_ref[0], cos, sin)
        elif kind == 2:
            blk = _norm_rope_t(blk, kn_ref[1], cos, sin)
        if kind < 2:
            cmp_rows.append(blk)
        else:
            lo = ATT_SEL + (slot - 2 * N_KV_B) * HEAD_DIM
            attb_ref[0, lo:lo + HEAD_DIM] = blk.astype(BF16)
        nsat_ref[0, slot * HEAD_DIM:(slot + 1) * HEAD_DIM] = blk
    for slot in range(2 * N_KV_B):
        lo = OFF_WIN + slot * HEAD_DIM
        blk = pt[lo:lo + HEAD_DIM]
        if slot // N_KV_B == 0:
            blk = _norm_rope_t(blk, kn_ref[2], cos, sin)
        wint_ref[0, slot * HEAD_DIM:(slot + 1) * HEAD_DIM] = blk
        attb_ref[0, ATT_WIN + slot * HEAD_DIM:ATT_WIN + (slot + 1) * HEAD_DIM] = blk.astype(BF16)
    gt_ref[0] = jax.nn.sigmoid(pt[OFF_GATE:OFF_GATE + LANES] + bg_ref[...])
    cmpin_ref[0] = jnp.concatenate(cmp_rows, axis=0).T.astype(BF16)


def _project(x, pos, norm_attn, wt, bg_col, qn_col, kn_cols, tile):
    nb, t, d = x.shape
    half = ROPE_HALF
    inv = ROPE_THETA ** (-jnp.arange(half, dtype=F32) / half)
    ang = inv[:, None] * pos.astype(F32)[None, :]
    cos = jnp.cos(ang)
    sin = jnp.sin(ang)
    grid = (nb, t // tile)
    row = lambda n: pl.BlockSpec((1, n, tile), lambda b, i: (b, 0, i))
    out_shape = (
        jax.ShapeDtypeStruct((nb, WIDTH_A, t), BF16),
        jax.ShapeDtypeStruct((nb, 2 * WIDTH_A, t), F32),
        jax.ShapeDtypeStruct((nb, WIDTH_B, t), BF16),
        jax.ShapeDtypeStruct((nb, 4 * KV_WIDTH_B, t), F32),
        jax.ShapeDtypeStruct((nb, 2 * KV_WIDTH_B, t), F32),
        jax.ShapeDtypeStruct((nb, LANES, t), F32),
        jax.ShapeDtypeStruct((nb, t, 2 * KV_WIDTH_B), BF16),
        jax.ShapeDtypeStruct((nb, ATT_ROWS, t), BF16),
    )
    return pl.pallas_call(
        _proj_kernel,
        grid=grid,
        in_specs=[
            pl.BlockSpec((1, tile, d), lambda b, i: (b, i, 0)),
            _const_spec((1, d)),
            _const_spec((PROJ_ROWS, d)),
            _const_spec((LANES, 1)),
            _const_spec((HEAD_DIM, 1)),
            _const_spec((N_BRANCH, HEAD_DIM, 1)),
            pl.BlockSpec((half, tile), lambda b, i: (0, i)),
            pl.BlockSpec((half, tile), lambda b, i: (0, i)),
        ],
        out_specs=(row(WIDTH_A), row(2 * WIDTH_A), row(WIDTH_B), row(4 * KV_WIDTH_B), row(2 * KV_WIDTH_B),
                   row(LANES), pl.BlockSpec((1, tile, 2 * KV_WIDTH_B), lambda b, i: (b, i, 0)), row(ATT_ROWS)),
        out_shape=out_shape,
        compiler_params=_params("parallel", "parallel"),
        name="project",
    )(x, norm_attn, wt, bg_col, qn_col, kn_cols, cos, sin)


def _compress_kernel(x_ref, w_ref, o_ref):
    o_ref[...] = _nn(x_ref[...], w_ref[...])


def _compress(xflat, wexp):
    rows, k = xflat.shape
    rb = min(rows, 256)
    return pl.pallas_call(
        _compress_kernel,
        grid=(rows // rb,),
        in_specs=[pl.BlockSpec((rb, k), lambda i: (i, 0)), _const_spec(wexp.shape)],
        out_specs=pl.BlockSpec((rb, wexp.shape[1]), lambda i: (i, 0)),
        out_shape=jax.ShapeDtypeStruct((rows, wexp.shape[1]), F32),
        compiler_params=_params("parallel"),
        name="compress",
    )(xflat, wexp)


def _log_sigmoids(z):
    lb = jnp.minimum(z, 0.0) - jnp.log(1.0 + jnp.exp(-jnp.abs(z)))
    return lb, lb - z


def _stick_weights(lb, lr, c, uu):
    rows = lb.shape[0]
    nblk = lb.shape[1] // LANES
    blk = lambda x, k: x[:, k * LANES:(k + 1) * LANES]
    stacked = jnp.concatenate([blk(lr, k) for k in range(nblk)], axis=0)
    au = _split_dot(stacked, uu)
    parts = [None] * nblk
    for k in reversed(range(nblk)):
        auk = au[k * rows:(k + 1) * rows]
        parts[k] = jnp.exp(blk(lb, k) + auk[:, :LANES] + c)
        c = c + auk[:, LANES:]
    return jnp.concatenate(parts, axis=1), c


def _sba_prompt_kernel(q_ref, kv_ref, uu_ref, o_ref, *, keys):
    qi = pl.program_id(1)
    heads = range(N_HEADS_A)
    rows = N_HEADS_A * QBLOCK
    qs = [q_ref[0, h * HEAD_DIM:(h + 1) * HEAD_DIM].astype(F32).T.astype(BF16) for h in heads]
    uu = uu_ref[...]
    tpos = qi * QBLOCK + lax.broadcasted_iota(I32, (rows, keys), 0) % QBLOCK
    kidx = lax.broadcasted_iota(I32, (rows, keys), 1)

    def step(p, carry, masked):
        c, accs = carry
        off = pl.multiple_of(p * keys, keys)
        z = jnp.concatenate(
            [_nn(qs[h], kv_ref[0, h * HEAD_DIM:(h + 1) * HEAD_DIM, pl.ds(off, keys)]) for h in heads], axis=0)
        lb, lr = _log_sigmoids(z * SCALE)
        if masked:
            valid = off + kidx < tpos
            lr = jnp.where(valid, lr, 0.0)
        a, c = _stick_weights(lb, lr, c, uu)
        if masked:
            a = jnp.where(valid, a, 0.0)
        a = a.astype(BF16)
        accs = tuple(
            accs[h] + _nt(kv_ref[0, WIDTH_A + h * HEAD_DIM:WIDTH_A + (h + 1) * HEAD_DIM, pl.ds(off, keys)],
                          a[h * QBLOCK:(h + 1) * QBLOCK]) for h in heads)
        return c, accs

    top = (qi * QBLOCK) // keys
    carry = (jnp.zeros((rows, LANES), F32), tuple(jnp.zeros((HEAD_DIM, QBLOCK), F32) for _ in heads))
    carry = step(top, carry, True)
    carry = lax.fori_loop(0, top, lambda i, cr: step(top - 1 - i, cr, False), carry)
    for h in heads:
        o_ref[0, h * HEAD_DIM:(h + 1) * HEAD_DIM] = carry[1][h]


def _sba_prompt(qat, attb, uu):
    nb, _, t = qat.shape
    nq = t // QBLOCK
    keys = 2 * QBLOCK if nq % 2 == 0 else QBLOCK
    return pl.pallas_call(
        functools.partial(_sba_prompt_kernel, keys=keys),
        grid=(nb, nq),
        in_specs=[
            pl.BlockSpec((1, WIDTH_A, QBLOCK), lambda b, i: (b, 0, i)),
            pl.BlockSpec((1, 2 * WIDTH_A, t), lambda b, i: (b, 0, 0)),
            _const_spec(uu.shape),
        ],
        out_specs=pl.BlockSpec((1, WIDTH_A, QBLOCK), lambda b, i: (b, 0, i)),
        out_shape=jax.ShapeDtypeStruct((nb, WIDTH_A, t), F32),
        compiler_params=_params("parallel", "arbitrary"),
        name="sba_prompt",
    )(qat, attb, uu)


def _sba_sample_kernel(pt_ref, q_ref, new_ref, *rest, n_steps):
    page_refs = rest[:PAGES_PER_STEP]
    uu_ref, o_ref, c_ref, acc_ref, qbd_ref = rest[PAGES_PER_STEP:]
    jj = pl.program_id(1)
    t_new = q_ref.shape[1]
    rows = N_HEADS_A * t_new

    def process(refs, valid):
        kt = jnp.concatenate([r[0, 0].reshape(WIDTH_A, LANES) for r in refs], axis=1).astype(BF16)
        vt = jnp.concatenate([r[0, 1].reshape(WIDTH_A, LANES) for r in refs], axis=1).astype(BF16)
        lb, lr = _log_sigmoids(_nn(qbd_ref[...], kt) * SCALE)
        if valid is not None:
            lr = jnp.where(valid, lr, 0.0)
        a, c = _stick_weights(lb, lr, c_ref[...], uu_ref[...])
        c_ref[...] = c
        if valid is not None:
            a = jnp.where(valid, a, 0.0)
        acc_ref[...] += _nt(a.astype(BF16), vt)

    @pl.when(jj == 0)
    def _():
        q = jnp.concatenate([q_ref[0].astype(F32)] * N_HEADS_A, axis=0)
        r = lax.broadcasted_iota(I32, q.shape, 0) // t_new
        cidx = lax.broadcasted_iota(I32, q.shape, 1) // HEAD_DIM
        qbd_ref[...] = jnp.where(r == cidx, q, 0.0).astype(BF16)
        c_ref[...] = jnp.zeros_like(c_ref)
        acc_ref[...] = jnp.zeros_like(acc_ref)
        tq = lax.broadcasted_iota(I32, (rows, LANES), 0) % t_new
        s = lax.broadcasted_iota(I32, (rows, LANES), 1)
        process([new_ref], s < tq)

    process(page_refs, None)

    @pl.when(jj == n_steps - 1)
    def _():
        acc = acc_ref[...]
        r = lax.broadcasted_iota(I32, acc.shape, 0) // t_new
        cidx = lax.broadcasted_iota(I32, acc.shape, 1) // HEAD_DIM
        acc = jnp.where(r == cidx, acc, 0.0)
        out = acc[0:t_new]
        for hd in range(1, N_HEADS_A):
            out = out + acc[hd * t_new:(hd + 1) * t_new]
        o_ref[0] = out


def _sba_sample(page_table, q, newt, cache_t, uu):
    nb, t_new, _ = q.shape
    n_pages = page_table.shape[1]
    n_steps = n_pages // PAGES_PER_STEP
    rows = N_HEADS_A * t_new

    def page_spec(i):
        def imap(b, j, pt):
            return (pt[b, n_pages - (j + 1) * PAGES_PER_STEP + i], 0, 0, 0, 0)
        return pl.BlockSpec((1, 2, N_HEADS_A, HEAD_DIM, LANES), imap)

    grid_spec = pltpu.PrefetchScalarGridSpec(
        num_scalar_prefetch=1,
        grid=(nb, n_steps),
        in_specs=[
            pl.BlockSpec((1, t_new, WIDTH_A), lambda b, j, pt: (b, 0, 0)),
            pl.BlockSpec((1, 2, N_HEADS_A, HEAD_DIM, LANES), lambda b, j, pt: (b, 0, 0, 0, 0)),
            *[page_spec(i) for i in range(PAGES_PER_STEP)],
            _const_spec(uu.shape),
        ],
        out_specs=pl.BlockSpec((1, t_new, WIDTH_A), lambda b, j, pt: (b, 0, 0)),
        scratch_shapes=[
            pltpu.VMEM((rows, LANES), F32),
            pltpu.VMEM((rows, WIDTH_A), F32),
            pltpu.VMEM((rows, WIDTH_A), BF16),
        ],
    )
    return pl.pallas_call(
        functools.partial(_sba_sample_kernel, n_steps=n_steps),
        grid_spec=grid_spec,
        out_shape=jax.ShapeDtypeStruct((nb, t_new, WIDTH_A), F32),
        compiler_params=_params("parallel", "arbitrary"),
        name="sba_sample",
    )(page_table, q, newt, *([cache_t] * PAGES_PER_STEP), uu)


def _select_blocks(imp, qpos, n_blocks, n_sel, block_axis):
    sb = lax.broadcasted_iota(I32, imp.shape, block_axis)
    imp = jnp.where(sb * L_SEL > qpos, NEG, imp)
    forced = (sb == 0) | (sb == qpos // L_SEL)
    imp = jnp.where(forced, SEL_BONUS, imp)
    rank = jnp.zeros(imp.shape, F32)
    for i in range(n_blocks):
        vi = imp[i:i + 1, :] if block_axis == 0 else imp[:, i:i + 1]
        beats = jnp.where(vi > imp, 1.0, jnp.where(vi == imp, jnp.where(sb > i, 1.0, 0.0), 0.0))
        rank = rank + beats
    return jnp.where((rank < n_sel) & (sb < n_blocks), 1.0, 0.0)


def _nsa_prompt_kernel(qt_ref, kcmp_ref, vcmpt_ref, kst_ref, vst_ref, kwt_ref, vwt_ref, gt_ref, e_ref, o_ref,
                       os_ref, *, span, n_var):
    g = pl.program_id(1)
    qi = pl.program_id(2)
    t = kst_ref.shape[2]
    var_keys = t // n_var
    ncb = kcmp_ref.shape[2]
    nsb = ncb // CMP_PER_SEL
    q0 = qi * QBLOCK
    tpos_l = q0 + lax.broadcasted_iota(I32, (1, QBLOCK), 1)
    nrow = GROUP_B * QBLOCK
    tpos_s = q0 + lax.broadcasted_iota(I32, (QBLOCK, 1), 0)
    tile = lambda x, axis: jnp.concatenate([x] * GROUP_B, axis=axis)
    q_all = jnp.concatenate(
        [qt_ref[0, m * HEAD_DIM:(m + 1) * HEAD_DIM].astype(F32).T for m in range(GROUP_B)], axis=0).astype(BF16)

    r = lax.broadcasted_iota(I32, (ncb, 1), 0)
    n = CMP_PER_SEL * (r % nsb) + r // nsb
    cvalid = ((n + 1) * L_CMP - 1) <= q0 + lax.broadcasted_iota(I32, (1, nrow), 1) % QBLOCK
    zc = jnp.where(cvalid, _nt(kcmp_ref[0, 0].astype(BF16), q_all) * SCALE, NEG)
    ex = jnp.exp(zc - jnp.max(zc, axis=0, keepdims=True))
    pc = ex / jnp.sum(ex, axis=0, keepdims=True)
    pc = jnp.where(cvalid, pc, 0.0)
    oc = _nn(vcmpt_ref[0, 0].astype(BF16), pc.astype(BF16))
    imp = pc[:, 0:QBLOCK]
    for m in range(1, GROUP_B):
        imp = imp + pc[:, m * QBLOCK:(m + 1) * QBLOCK]
    imp = imp[0:nsb] + imp[nsb:ncb]
    sel = _select_blocks(imp, tpos_l, nsb, min(N_SEL, nsb), 0).astype(BF16)

    q_scaled = (q_all.astype(F32) * SCALE).astype(BF16)

    def branch(kt, vt, bias):
        zm = _nn(q_scaled, kt) + tile(bias, 0)
        p = jnp.exp(zm - jnp.max(zm, axis=-1, keepdims=True)).astype(BF16)
        ones = jnp.ones((8, p.shape[1]), BF16)
        return _nt(vt, p) / _nt(ones, p)[0:1]

    for v in range(n_var):
        nk = (v + 1) * var_keys

        @pl.when(q0 // var_keys == v)
        def _():
            keymask = _tn(sel, e_ref[:, 0:nk])
            kpos = lax.broadcasted_iota(I32, (1, nk), 1)
            bias = jnp.where(keymask > 0.5, jnp.where(kpos <= tpos_s, 0.0, NEG), NEG)
            os_ref[...] = branch(kst_ref[0, :, 0:nk], vst_ref[0, :, 0:nk], bias)

    ws = pl.multiple_of(jnp.maximum(qi - WINDOW // QBLOCK, 0) * QBLOCK, QBLOCK)
    wpos = ws + lax.broadcasted_iota(I32, (1, span), 1)
    wbias = jnp.where(wpos <= tpos_s, jnp.where(wpos > tpos_s - WINDOW, 0.0, NEG), NEG)
    ow = branch(kwt_ref[0, :, pl.ds(ws, span)], vwt_ref[0, :, pl.ds(ws, span)], wbias)
    for m in range(GROUP_B):
        hb = g * GROUP_B + m
        gc = gt_ref[0, pl.ds(hb, 1), :]
        gs = gt_ref[0, pl.ds(N_HEADS_B + hb, 1), :]
        gw = gt_ref[0, pl.ds(2 * N_HEADS_B + hb, 1), :]
        cols = slice(m * QBLOCK, (m + 1) * QBLOCK)
        o_ref[0, m * HEAD_DIM:(m + 1) * HEAD_DIM] = gc * oc[:, cols] + gs * os_ref[:, cols] + gw * ow[:, cols]


def _nsa_prompt(qbt, kcmp, vcmpt, attb, gt, e):
    nb, _, t = qbt.shape
    nq = t // QBLOCK
    ncb = kcmp.shape[2]
    span = min(WINDOW + QBLOCK, t)
    n_var = 4 if nq % 4 == 0 else 1
    kv = lambda base, slot: pl.BlockSpec(
        (1, HEAD_DIM, t), lambda b, g, i: (b, base // HEAD_DIM + slot * N_KV_B + g, 0))
    return pl.pallas_call(
        functools.partial(_nsa_prompt_kernel, span=span, n_var=n_var),
        grid=(nb, N_KV_B, nq),
        in_specs=[
            pl.BlockSpec((1, GROUP_B * HEAD_DIM, QBLOCK), lambda b, g, i: (b, g, i)),
            pl.BlockSpec((1, 1, ncb, HEAD_DIM), lambda b, g, i: (b, g, 0, 0)),
            pl.BlockSpec((1, 1, HEAD_DIM, ncb), lambda b, g, i: (b, g, 0, 0)),
            kv(ATT_SEL, 0), kv(ATT_SEL, 1), kv(ATT_WIN, 0), kv(ATT_WIN, 1),
            pl.BlockSpec((1, LANES, QBLOCK), lambda b, g, i: (b, 0, i)),
            _const_spec(e.shape),
        ],
        out_specs=pl.BlockSpec((1, GROUP_B * HEAD_DIM, QBLOCK), lambda b, g, i: (b, g, i)),
        out_shape=jax.ShapeDtypeStruct((nb, WIDTH_B, t), F32),
        scratch_shapes=[pltpu.VMEM((HEAD_DIM, GROUP_B * QBLOCK), F32)],
        compiler_params=_params("parallel", "parallel", "arbitrary"),
        name="nsa_prompt",
    )(qbt, kcmp, vcmpt, attb, attb, attb, attb, gt, e)


def _nsa_sample_kernel(pt_ref, q_ref, new_ref, win_ref, wnew_ref, g_ref, *rest, n_steps, past):
    page_refs = rest[:PAGES_PER_STEP]
    wexp_ref, e_ref, perm_ref, o_ref, xs_ref, kst_ref, vst_ref, cmp_ref = rest[PAGES_PER_STEP:]
    jj = pl.program_id(1)
    t_new = q_ref.shape[1]
    cw = 2 * KV_WIDTH_B
    pair_rows = 2 * LANES // L_CMP

    for pr in range(PAGES_PER_STEP // 2):
        pages = jnp.concatenate([page_refs[2 * pr + i][0, 0:2].reshape(cw, LANES) for i in range(2)], axis=1)
        y = _nt(perm_ref[...], pages.astype(BF16))
        row0 = pl.multiple_of((jj * (PAGES_PER_STEP // 2) + pr) * pair_rows, pair_rows)
        for l in range(L_CMP):
            xs_ref[l, pl.ds(row0, pair_rows), :] = y[l * pair_rows:(l + 1) * pair_rows]
    for i, ref in enumerate(page_refs):
        off = pl.multiple_of((jj * PAGES_PER_STEP + i) * LANES, LANES)
        for g in range(N_KV_B):
            kst_ref[g, :, pl.ds(off, LANES)] = ref[0, 2, g].astype(BF16)
            vst_ref[g, :, pl.ds(off, LANES)] = ref[0, 3, g].astype(BF16)

    @pl.when(jj == n_steps - 1)
    def _():
        ncb = past // L_CMP
        nsb_past = past // L_SEL
        total = past + LANES
        for g in range(N_KV_B):
            kst_ref[g, :, past:total] = new_ref[0, 2, g].astype(BF16)
            vst_ref[g, :, past:total] = new_ref[0, 3, g].astype(BF16)

        halves = range(cw // LANES)

        def cbody(i, accs):
            out = []
            for par, acc in enumerate(accs):
                l = 2 * i + par
                w = wexp_ref[pl.ds(pl.multiple_of(l * cw, cw), cw), :]
                out.append(acc + _nn(xs_ref[l].astype(BF16), w))
            return tuple(out)

        zero = jnp.zeros((ncb, cw), F32)
        cmp = sum(lax.fori_loop(0, L_CMP // 2, cbody, (zero, zero)))
        for hf in halves:
            cmp_ref[hf] = cmp[:, hf * LANES:(hf + 1) * LANES]
        cmp_perm = jnp.concatenate(
            [jnp.concatenate([cmp_ref[hf, pl.ds(par, nsb_past, stride=CMP_PER_SEL), :] for hf in halves], axis=1)
             for par in range(CMP_PER_SEL)], axis=0)

        rows = GROUP_B * t_new
        tq = lax.broadcasted_iota(I32, (rows, 1), 0) % t_new
        qpos = past + tq
        qpos_t = past + lax.broadcasted_iota(I32, (t_new, 1), 0)
        r = lax.broadcasted_iota(I32, (1, ncb), 1)
        n = CMP_PER_SEL * (r % nsb_past) + r // nsb_past
        cvalid = ((n + 1) * L_CMP - 1) <= qpos
        kpos = lax.broadcasted_iota(I32, (1, total), 1)
        wlen = win_ref.shape[4]
        wl = lax.broadcasted_iota(I32, (1, wlen + LANES), 1)
        wpos = jnp.where(wl < wlen, past - wlen + wl, past + wl - wlen)
        tile = lambda x: jnp.concatenate([x] * GROUP_B, axis=0)
        wbias = tile(jnp.where(wpos <= qpos_t, jnp.where(wpos > qpos_t - WINDOW, 0.0, NEG), NEG))
        nsb = nsb_past + 1
        nsb_pad = e_ref.shape[0]
        gates = g_ref[0]

        qall = q_ref[0].astype(F32)
        qgs, ocs, sels = [], [], []
        for g in range(N_KV_B):
            qg = jnp.concatenate(
                [qall[:, (g * GROUP_B + m) * HEAD_DIM:(g * GROUP_B + m + 1) * HEAD_DIM] for m in range(GROUP_B)],
                axis=0).astype(BF16)
            qgs.append(qg)
            kcmp = cmp_perm[:, g * HEAD_DIM:(g + 1) * HEAD_DIM].astype(BF16)
            vcmp = cmp_perm[:, KV_WIDTH_B + g * HEAD_DIM:KV_WIDTH_B + (g + 1) * HEAD_DIM].astype(BF16)
            zc = jnp.where(cvalid, _nt(qg, kcmp) * SCALE, NEG)
            ex = jnp.exp(zc - jnp.max(zc, axis=-1, keepdims=True))
            pc = ex / jnp.sum(ex, axis=-1, keepdims=True)
            pc = jnp.where(cvalid, pc, 0.0)
            ocs.append(_nn(pc.astype(BF16), vcmp))
            imp = pc[0:t_new]
            for m in range(1, GROUP_B):
                imp = imp + pc[m * t_new:(m + 1) * t_new]
            imp = imp[:, 0:nsb_past] + imp[:, nsb_past:ncb]
            imp = jnp.concatenate([imp, jnp.zeros((t_new, nsb_pad - nsb_past), F32)], axis=1)
            sels.append(_select_blocks(imp, qpos_t, nsb, min(N_SEL, nsb), 1))

        keymasks = _nn(jnp.concatenate(sels, axis=0).astype(BF16), e_ref[...])
        for g in range(N_KV_B):
            qg = qgs[g]
            oc = ocs[g]
            keymask = keymasks[g * t_new:(g + 1) * t_new]
            sbias = tile(jnp.where(keymask > 0.5, jnp.where(kpos <= qpos_t, 0.0, NEG), NEG))

            def branch(kt, vt, bias):
                zm = _nn(qg, kt) * SCALE + bias
                p = jnp.exp(zm - jnp.max(zm, axis=-1, keepdims=True))
                l = jnp.sum(p, axis=-1, keepdims=True)
                return _nt(p.astype(BF16), vt) / l

            os_ = branch(kst_ref[g], vst_ref[g], sbias)
            kwt = jnp.concatenate([win_ref[0, 0, g], wnew_ref[0, 0, g]], axis=1).astype(BF16)
            vwt = jnp.concatenate([win_ref[0, 1, g], wnew_ref[0, 1, g]], axis=1).astype(BF16)
            ow = branch(kwt, vwt, wbias)
            for m in range(GROUP_B):
                hb = g * GROUP_B + m
                sl = slice(m * t_new, (m + 1) * t_new)
                o_ref[0, :, hb * HEAD_DIM:(hb + 1) * HEAD_DIM] = (
                    gates[:, hb:hb + 1] * oc[sl]
                    + gates[:, N_HEADS_B + hb:N_HEADS_B + hb + 1] * os_[sl]
                    + gates[:, 2 * N_HEADS_B + hb:2 * N_HEADS_B + hb + 1] * ow[sl])


def _pair_permutation():
    blocks = LANES // L_CMP
    l, page, n = np.meshgrid(np.arange(L_CMP), np.arange(2), np.arange(blocks), indexing="ij")
    perm = np.zeros((2 * LANES, 2 * LANES), np.float32)
    perm[(l * 2 * blocks + page * blocks + n).ravel(), (page * LANES + n * L_CMP + l).ravel()] = 1.0
    return jnp.asarray(perm, dtype=BF16)


def _nsa_sample(page_table, q, newt, win_t, wnew_t, gates, cache_t, wexp, e):
    nb, t_new, _ = q.shape
    perm = _pair_permutation()
    n_pages = page_table.shape[1]
    n_steps = n_pages // PAGES_PER_STEP
    past = n_pages * LANES
    wlen = win_t.shape[4]
    cw = 2 * KV_WIDTH_B

    def page_spec(i):
        def imap(b, j, pt):
            return (pt[b, j * PAGES_PER_STEP + i], 0, 0, 0, 0)
        return pl.BlockSpec((1, 4, N_KV_B, HEAD_DIM, LANES), imap)

    grid_spec = pltpu.PrefetchScalarGridSpec(
        num_scalar_prefetch=1,
        grid=(nb, n_steps),
        in_specs=[
            pl.BlockSpec((1, t_new, WIDTH_B), lambda b, j, pt: (b, 0, 0)),
            pl.BlockSpec((1, 4, N_KV_B, HEAD_DIM, LANES), lambda b, j, pt: (b, 0, 0, 0, 0)),
            pl.BlockSpec((1, 2, N_KV_B, HEAD_DIM, wlen), lambda b, j, pt: (b, 0, 0, 0, 0)),
            pl.BlockSpec((1, 2, N_KV_B, HEAD_DIM, LANES), lambda b, j, pt: (b, 0, 0, 0, 0)),
            pl.BlockSpec((1, t_new, LANES), lambda b, j, pt: (b, 0, 0)),
            *[page_spec(i) for i in range(PAGES_PER_STEP)],
            _const_spec(wexp.shape),
            _const_spec(e.shape),
            _const_spec(perm.shape),
        ],
        out_specs=pl.BlockSpec((1, t_new, WIDTH_B), lambda b, j, pt: (b, 0, 0)),
        scratch_shapes=[
            pltpu.VMEM((L_CMP, past // L_CMP, cw), F32),
            pltpu.VMEM((N_KV_B, HEAD_DIM, past + LANES), BF16),
            pltpu.VMEM((N_KV_B, HEAD_DIM, past + LANES), BF16),
            pltpu.VMEM((cw // LANES, past // L_CMP, LANES), F32),
        ],
    )
    return pl.pallas_call(
        functools.partial(_nsa_sample_kernel, n_steps=n_steps, past=past),
        grid_spec=grid_spec,
        out_shape=jax.ShapeDtypeStruct((nb, t_new, WIDTH_B), F32),
        compiler_params=_params("parallel", "arbitrary"),
        name="nsa_sample",
    )(page_table, q, newt, win_t, wnew_t, gates, *([cache_t] * PAGES_PER_STEP), wexp, e, perm)


def _combine_kernel(x_ref, oat_ref, obt_ref, na_ref, nb_ref, wout_ref, nm_ref, wup_ref, wdown_ref, y_ref):
    def norm_t(o, gcol):
        return o * lax.rsqrt(jnp.mean(o * o, axis=0, keepdims=True) + EPS) * gcol

    mixed_t = jnp.concatenate([norm_t(oat_ref[0], na_ref[...]), norm_t(obt_ref[0], nb_ref[...])], axis=0)
    x1 = x_ref[0] + _tn(mixed_t.astype(BF16), wout_ref[...])
    h = x1 * lax.rsqrt(jnp.mean(x1 * x1, axis=-1, keepdims=True) + EPS) * nm_ref[...]
    up = jnp.square(jnp.maximum(_nn(h.astype(BF16), wup_ref[...]), 0.0))
    y_ref[0] = x1 + _nn(up.astype(BF16), wdown_ref[...])


def _combine(x, oat, obt, na_col, nb_col, wout, nm, wup, wdown, tile):
    nb, t, d = x.shape
    return pl.pallas_call(
        _combine_kernel,
        grid=(nb, t // tile),
        in_specs=[
            pl.BlockSpec((1, tile, d), lambda b, i: (b, i, 0)),
            pl.BlockSpec((1, WIDTH_A, tile), lambda b, i: (b, 0, i)),
            pl.BlockSpec((1, WIDTH_B, tile), lambda b, i: (b, 0, i)),
            _const_spec(na_col.shape), _const_spec(nb_col.shape), _const_spec(wout.shape),
            _const_spec(nm.shape), _const_spec(wup.shape), _const_spec(wdown.shape),
        ],
        out_specs=pl.BlockSpec((1, tile, d), lambda b, i: (b, i, 0)),
        out_shape=jax.ShapeDtypeStruct((nb, t, d), F32),
        compiler_params=_params("parallel", "parallel"),
        name="combine_mlp",
    )(x, oat, obt, na_col, nb_col, wout, nm, wup, wdown)


def _expanded_cmp_weights(w_cmp_k, w_cmp_v):
    wk = w_cmp_k.reshape(L_CMP, HEAD_DIM, HEAD_DIM)
    wv = w_cmp_v.reshape(L_CMP, HEAD_DIM, HEAD_DIM)
    slots = [wk] * N_KV_B + [wv] * N_KV_B
    ns = len(slots)
    zero = jnp.zeros_like(wk)
    rows = [jnp.concatenate([slots[i] if i == j else zero for j in range(ns)], axis=2) for i in range(ns)]
    return jnp.concatenate(rows, axis=1).reshape(L_CMP * ns * HEAD_DIM, ns * HEAD_DIM).astype(BF16)


def _expansion(n_rows, n_keys):
    j = np.arange(n_rows)[:, None]
    s = np.arange(n_keys)[None, :]
    return jnp.asarray(s // L_SEL == j, dtype=BF16)


def _kv_out(xt, n_slots, n_heads):
    b, _, t = xt.shape
    return xt.reshape(b, n_slots, n_heads, HEAD_DIM, t).transpose(0, 4, 1, 2, 3)[None]


def kernel(x_prompt, x_sample, cache_sba_kv, cache_nsa_kv, state_win_kv, page_table, norm_attn, w_in, b_gate,
           q_norm_b, k_norm_b, w_cmp_k, w_cmp_v, norm_out_a, norm_out_b, w_out, norm_mlp, w_up, w_down):
    assert norm_attn.shape[0] == 1, "single trunk layer"
    nb, t, d = x_prompt.shape
    db, t_new, _ = x_sample.shape
    n_pages = page_table.shape[1]
    past = n_pages * LANES
    assert cache_sba_kv.shape[2] == LANES and t % QBLOCK == 0 and t_new < L_CMP
    assert n_pages % PAGES_PER_STEP == 0 and t >= WINDOW and state_win_kv.shape[2] == WINDOW

    wt = jnp.pad(jnp.transpose(w_in[0]), ((0, PROJ_ROWS - w_in.shape[2]), (0, 0))).astype(BF16)
    bg_col = jnp.pad(b_gate[0], (0, LANES - N_GATE))[:, None]
    qn_col = q_norm_b[0][:, None]
    kn_cols = k_norm_b[0][:, :, None]
    wexp = _expanded_cmp_weights(w_cmp_k[0], w_cmp_v[0])
    na_col = norm_out_a[0][:, None]
    nb_col = norm_out_b[0][:, None]
    wout = w_out[0].astype(BF16)
    wup = w_up[0].astype(BF16)
    wdown = w_down[0].astype(BF16)
    sidx = np.arange(LANES)
    u2 = np.concatenate([sidx[:, None] > sidx[None, :], np.ones((LANES, LANES), bool)], axis=1)
    uu = jnp.asarray(np.concatenate([u2, u2], axis=0), dtype=BF16)

    tile = min(256, t)
    qat, sbat, qbt, nsat, wint, gt, cmpin, attb = _project(
        x_prompt, jnp.arange(t), norm_attn, wt, bg_col, qn_col, kn_cols, tile)
    ncb = t // L_CMP
    nsb = ncb // CMP_PER_SEL
    cw = 2 * KV_WIDTH_B
    xflat = cmpin.reshape(nb, nsb, CMP_PER_SEL, L_CMP * cw).transpose(0, 2, 1, 3).reshape(nb * ncb, L_CMP * cw)
    cmp = _compress(xflat, wexp).reshape(nb, ncb, 2 * N_KV_B, HEAD_DIM)
    kcmp = cmp[:, :, 0:N_KV_B].transpose(0, 2, 1, 3)
    vcmpt = cmp[:, :, N_KV_B:].transpose(0, 2, 3, 1)
    oat = _sba_prompt(qat, attb, uu)
    obt = _nsa_prompt(qbt, kcmp, vcmpt, attb, gt, _expansion(nsb, t))
    y_prompt = _combine(x_prompt, oat, obt, na_col, nb_col, wout, norm_mlp, wup, wdown, tile)

    cols = db * t_new
    pos_s = past + jnp.arange(cols) % t_new
    qat_s, sbat_s, qbt_s, nsat_s, wint_s, gt_s, _, _ = _project(
        x_sample.reshape(1, cols, d), pos_s, norm_attn, wt, bg_col, qn_col, kn_cols, min(256, cols))

    def per_batch_rows(xt):
        return xt[0].reshape(xt.shape[1], db, t_new).transpose(1, 2, 0)

    def per_batch_padded(xt, n_slots, n_heads):
        x5 = xt[0].reshape(n_slots, n_heads, HEAD_DIM, db, t_new).transpose(3, 0, 1, 2, 4)
        return jnp.pad(x5, ((0, 0),) * 4 + ((0, LANES - t_new),))

    sba_cache_t = jnp.transpose(cache_sba_kv[0], (0, 2, 3, 4, 1))
    nsa_cache_t = jnp.transpose(cache_nsa_kv[0], (0, 2, 3, 4, 1))
    win_t = jnp.transpose(state_win_kv[0], (0, 2, 3, 4, 1))

    oa_s = _sba_sample(page_table, per_batch_rows(qat_s), per_batch_padded(sbat_s, 2, N_HEADS_A), sba_cache_t, uu)
    nsb_pad = -(-(past // L_SEL + 1) // LANES) * LANES
    ob_s = _nsa_sample(page_table, per_batch_rows(qbt_s), per_batch_padded(nsat_s, 4, N_KV_B), win_t,
                       per_batch_padded(wint_s, 2, N_KV_B), per_batch_rows(gt_s), nsa_cache_t, wexp,
                       _expansion(nsb_pad, past + LANES))
    to_t = lambda o: o.reshape(cols, o.shape[2]).T[None]
    y_sample = _combine(x_sample.reshape(1, cols, d), to_t(oa_s), to_t(ob_s), na_col, nb_col, wout, norm_mlp,
                        wup, wdown, min(256, cols)).reshape(db, t_new, d)

    def sample_kv(xt, n_slots, n_heads):
        return xt[0].reshape(n_slots, n_heads, HEAD_DIM, db, t_new).transpose(3, 4, 0, 1, 2)[None]

    win_new = sample_kv(wint_s, 2, N_KV_B)
    win_s = jnp.concatenate([state_win_kv[:, :, t_new:], win_new], axis=2)
    return (y_prompt, y_sample, _kv_out(sbat, 2, N_HEADS_A), _kv_out(nsat, 4, N_KV_B),
            _kv_out(wint[:, :, t - WINDOW:], 2, N_KV_B), sample_kv(sbat_s, 2, N_HEADS_A),
            sample_kv(nsat_s, 4, N_KV_B), win_s)
```

```python
import functools

import jax
import jax.numpy as jnp
import numpy as np
from jax import lax
from jax.experimental import pallas as pl
from jax.experimental.pallas import tpu as pltpu

F32 = jnp.float32
BF16 = jnp.bfloat16
I32 = jnp.int32

HEAD_DIM = 64
N_HEADS_A = 8
N_HEADS_B = 8
N_KV_B = 2
GROUP_B = N_HEADS_B // N_KV_B
WIDTH_A = N_HEADS_A * HEAD_DIM
WIDTH_B = N_HEADS_B * HEAD_DIM
KV_WIDTH_B = N_KV_B * HEAD_DIM
ROPE_DIM = HEAD_DIM // 4
ROPE_HALF = ROPE_DIM // 2
ROPE_THETA = 500000.0
L_CMP = 32
L_SEL = 64
CMP_PER_SEL = L_SEL // L_CMP
N_SEL = 16
WINDOW = 512
QBLOCK = 128
N_BRANCH = 3
EPS = 1e-6
NEG = -1e30
SEL_BONUS = 1e4
SCALE = HEAD_DIM ** -0.5

LANES = 128
PAGES_PER_STEP = 16
VMEM_LIMIT = 56 * 1024 * 1024

OFF_QA = 0
OFF_SBA = WIDTH_A
OFF_QB = 3 * WIDTH_A
OFF_NSA = OFF_QB + WIDTH_B
OFF_WIN = OFF_NSA + 4 * KV_WIDTH_B
OFF_GATE = OFF_WIN + 2 * KV_WIDTH_B
N_GATE = N_BRANCH * N_HEADS_B
PROJ_ROWS = OFF_GATE + LANES

ATT_SEL = 2 * WIDTH_A
ATT_WIN = ATT_SEL + 2 * KV_WIDTH_B
ATT_ROWS = ATT_WIN + 2 * KV_WIDTH_B


def _nt(a, b):
    return lax.dot_general(a, b, (((1,), (1,)), ((), ())), preferred_element_type=F32)


def _tn(a, b):
    return lax.dot_general(a, b, (((0,), (0,)), ((), ())), preferred_element_type=F32)


def _nn(a, b):
    return jnp.dot(a, b, preferred_element_type=F32)


def _split_dot(x, uu):
    hi = x.astype(BF16)
    lo = (x - hi.astype(F32)).astype(BF16)
    return _nn(jnp.concatenate([hi, lo], axis=1), uu)


def _params(*sem):
    return pltpu.CompilerParams(dimension_semantics=sem, vmem_limit_bytes=VMEM_LIMIT)


def _const_spec(shape):
    nd = len(shape)
    return pl.BlockSpec(shape, lambda *_: (0,) * nd, pipeline_mode=pl.Buffered(1))


def _norm_rope_t(xt, gcol, cos, sin):
    ms = jnp.mean(xt * xt, axis=0, keepdims=True)
    y = xt * lax.rsqrt(ms + EPS) * gcol
    x1 = y[0:ROPE_HALF]
    x2 = y[ROPE_HALF:ROPE_DIM]
    return jnp.concatenate([x1 * cos - x2 * sin, x2 * cos + x1 * sin, y[ROPE_DIM:]], axis=0)


def _proj_kernel(x_ref, g_ref, wt_ref, bg_ref, qn_ref, kn_ref, cos_ref, sin_ref,
                 qat_ref, sbat_ref, qbt_ref, nsat_ref, wint_ref, gt_ref, cmpin_ref, attb_ref):
    x = x_ref[0]
    ms = jnp.mean(x * x, axis=-1, keepdims=True)
    h = (x * lax.rsqrt(ms + EPS)) * g_ref[...]
    pt = _nt(wt_ref[...], h.astype(BF16))
    cos = cos_ref[...]
    sin = sin_ref[...]
    qat_ref[0] = pt[OFF_QA:OFF_QA + WIDTH_A].astype(BF16)
    sbat_ref[0] = pt[OFF_SBA:OFF_SBA + 2 * WIDTH_A]
    attb_ref[0, 0:2 * WIDTH_A] = pt[OFF_SBA:OFF_SBA + 2 * WIDTH_A].astype(BF16)
    qn = qn_ref[...]
    for hd in range(N_HEADS_B):
        lo = OFF_QB + hd * HEAD_DIM
        qbt_ref[0, hd * HEAD_DIM:(hd + 1) * HEAD_DIM] = _norm_rope_t(pt[lo:lo + HEAD_DIM], qn, cos, sin).astype(BF16)
    cmp_rows = []
    for slot in range(4 * N_KV_B):
        lo = OFF_NSA + slot * HEAD_DIM
        blk = pt[lo:lo + HEAD_DIM]
        kind = slot // N_KV_B
        if kind == 0:
            blk = _norm_rope_t(blk, kn_ref[0], cos, sin)
        elif kind == 2:
            blk = _norm_rope_t(blk, kn_ref[1], cos, sin)
        if kind < 2:
            cmp_rows.append(blk)
        else:
            lo = ATT_SEL + (slot - 2 * N_KV_B) * HEAD_DIM
            attb_ref[0, lo:lo + HEAD_DIM] = blk.astype(BF16)
        nsat_ref[0, slot * HEAD_DIM:(slot + 1) * HEAD_DIM] = blk
    for slot in range(2 * N_KV_B):
        lo = OFF_WIN + slot * HEAD_DIM
        blk = pt[lo:lo + HEAD_DIM]
        if slot // N_KV_B == 0:
            blk = _norm_rope_t(blk, kn_ref[2], cos, sin)
        wint_ref[0, slot * HEAD_DIM:(slot + 1) * HEAD_DIM] = blk
        attb_ref[0, ATT_WIN + slot * HEAD_DIM:ATT_WIN + (slot + 1) * HEAD_DIM] = blk.astype(BF16)
    gt_ref[0] = jax.nn.sigmoid(pt[OFF_GATE:OFF_GATE + LANES] + bg_ref[...])
    cmpin_ref[0] = jnp.concatenate(cmp_rows, axis=0).T.astype(BF16)


def _project(x, pos, norm_attn, wt, bg_col, qn_col, kn_cols, tile):
    nb, t, d = x.shape
    half = ROPE_HALF
    inv = ROPE_THETA ** (-jnp.arange(half, dtype=F32) / half)
    ang = inv[:, None] * pos.astype(F32)[None, :]
    cos = jnp.cos(ang)
    sin = jnp.sin(ang)
    grid = (nb, t // tile)
    row = lambda n: pl.BlockSpec((1, n, tile), lambda b, i: (b, 0, i))
    out_shape = (
        jax.ShapeDtypeStruct((nb, WIDTH_A, t), BF16),
        jax.ShapeDtypeStruct((nb, 2 * WIDTH_A, t), F32),
        jax.ShapeDtypeStruct((nb, WIDTH_B, t), BF16),
        jax.ShapeDtypeStruct((nb, 4 * KV_WIDTH_B, t), F32),
        jax.ShapeDtypeStruct((nb, 2 * KV_WIDTH_B, t), F32),
        jax.ShapeDtypeStruct((nb, LANES, t), F32),
        jax.ShapeDtypeStruct((nb, t, 2 * KV_WIDTH_B), BF16),
        jax.ShapeDtypeStruct((nb, ATT_ROWS, t), BF16),
    )
    return pl.pallas_call(
        _proj_kernel,
        grid=grid,
        in_specs=[
            pl.BlockSpec((1, tile, d), lambda b, i: (b, i, 0)),
            _const_spec((1, d)),
            _const_spec((PROJ_ROWS, d)),
            _const_spec((LANES, 1)),
            _const_spec((HEAD_DIM, 1)),
            _const_spec((N_BRANCH, HEAD_DIM, 1)),
            pl.BlockSpec((half, tile), lambda b, i: (0, i)),
            pl.BlockSpec((half, tile), lambda b, i: (0, i)),
        ],
        out_specs=(row(WIDTH_A), row(2 * WIDTH_A), row(WIDTH_B), row(4 * KV_WIDTH_B), row(2 * KV_WIDTH_B),
                   row(LANES), pl.BlockSpec((1, tile, 2 * KV_WIDTH_B), lambda b, i: (b, i, 0)), row(ATT_ROWS)),
        out_shape=out_shape,
        compiler_params=_params("parallel", "parallel"),
        name="project",
    )(x, norm_attn, wt, bg_col, qn_col, kn_cols, cos, sin)


def _compress_kernel(x_ref, w_ref, o_ref):
    o_ref[...] = _nn(x_ref[...], w_ref[...])


def _compress(xflat, wexp):
    rows, k = xflat.shape
    rb = min(rows, 256)
    return pl.pallas_call(
        _compress_kernel,
        grid=(rows // rb,),
        in_specs=[pl.BlockSpec((rb, k), lambda i: (i, 0)), _const_spec(wexp.shape)],
        out_specs=pl.BlockSpec((rb, wexp.shape[1]), lambda i: (i, 0)),
        out_shape=jax.ShapeDtypeStruct((rows, wexp.shape[1]), F32),
        compiler_params=_params("parallel"),
        name="compress",
    )(xflat, wexp)


def _log_sigmoids(z):
    lb = jnp.minimum(z, 0.0) - jnp.log(1.0 + jnp.exp(-jnp.abs(z)))
    return lb, lb - z


def _stick_weights(lb, lr, c, uu):
    rows = lb.shape[0]
    nblk = lb.shape[1] // LANES
    blk = lambda x, k: x[:, k * LANES:(k + 1) * LANES]
    stacked = jnp.concatenate([blk(lr, k) for k in range(nblk)], axis=0)
    au = _split_dot(stacked, uu)
    parts = [None] * nblk
    for k in reversed(range(nblk)):
        auk = au[k * rows:(k + 1) * rows]
        parts[k] = jnp.exp(blk(lb, k) + auk[:, :LANES] + c)
        c = c + auk[:, LANES:]
    return jnp.concatenate(parts, axis=1), c


def _sba_prompt_kernel(q_ref, kv_ref, uu_ref, o_ref, *, keys):
    qi = pl.program_id(1)
    heads = range(N_HEADS_A)
    rows = N_HEADS_A * QBLOCK
    qs = [q_ref[0, h * HEAD_DIM:(h + 1) * HEAD_DIM].astype(F32).T.astype(BF16) for h in heads]
    uu = uu_ref[...]
    tpos = qi * QBLOCK + lax.broadcasted_iota(I32, (rows, keys), 0) % QBLOCK
    kidx = lax.broadcasted_iota(I32, (rows, keys), 1)

    def step(p, carry, masked):
        c, accs = carry
        off = pl.multiple_of(p * keys, keys)
        z = jnp.concatenate(
            [_nn(qs[h], kv_ref[0, h * HEAD_DIM:(h + 1) * HEAD_DIM, pl.ds(off, keys)]) for h in heads], axis=0)
        lb, lr = _log_sigmoids(z * SCALE)
        if masked:
            valid = off + kidx < tpos
            lr = jnp.where(valid, lr, 0.0)
        a, c = _stick_weights(lb, lr, c, uu)
        if masked:
            a = jnp.where(valid, a, 0.0)
        a = a.astype(BF16)
        accs = tuple(
            accs[h] + _nt(kv_ref[0, WIDTH_A + h * HEAD_DIM:WIDTH_A + (h + 1) * HEAD_DIM, pl.ds(off, keys)],
                          a[h * QBLOCK:(h + 1) * QBLOCK]) for h in heads)
        return c, accs

    top = (qi * QBLOCK) // keys
    carry = (jnp.zeros((rows, LANES), F32), tuple(jnp.zeros((HEAD_DIM, QBLOCK), F32) for _ in heads))
    carry = step(top, carry, True)
    carry = lax.fori_loop(0, top, lambda i, cr: step(top - 1 - i, cr, False), carry)
    for h in heads:
        o_ref[0, h * HEAD_DIM:(h + 1) * HEAD_DIM] = carry[1][h]


def _sba_prompt(qat, attb, uu):
    nb, _, t = qat.shape
    nq = t // QBLOCK
    keys = 2 * QBLOCK if nq % 2 == 0 else QBLOCK
    return pl.pallas_call(
        functools.partial(_sba_prompt_kernel, keys=keys),
        grid=(nb, nq),
        in_specs=[
            pl.BlockSpec((1, WIDTH_A, QBLOCK), lambda b, i: (b, 0, i)),
            pl.BlockSpec((1, 2 * WIDTH_A, t), lambda b, i: (b, 0, 0)),
            _const_spec(uu.shape),
        ],
        out_specs=pl.BlockSpec((1, WIDTH_A, QBLOCK), lambda b, i: (b, 0, i)),
        out_shape=jax.ShapeDtypeStruct((nb, WIDTH_A, t), F32),
        compiler_params=_params("parallel", "arbitrary"),
        name="sba_prompt",
    )(qat, attb, uu)


def _sba_sample_kernel(pt_ref, q_ref, new_ref, *rest, n_steps):
    page_refs = rest[:PAGES_PER_STEP]
    uu_ref, o_ref, c_ref, acc_ref, qbd_ref = rest[PAGES_PER_STEP:]
    jj = pl.program_id(1)
    t_new = q_ref.shape[1]
    rows = N_HEADS_A * t_new

    def process(refs, valid):
        kt = jnp.concatenate([r[0, 0].reshape(WIDTH_A, LANES) for r in refs], axis=1).astype(BF16)
        vt = jnp.concatenate([r[0, 1].reshape(WIDTH_A, LANES) for r in refs], axis=1).astype(BF16)
        lb, lr = _log_sigmoids(_nn(qbd_ref[...], kt) * SCALE)
        if valid is not None:
            lr = jnp.where(valid, lr, 0.0)
        a, c = _stick_weights(lb, lr, c_ref[...], uu_ref[...])
        c_ref[...] = c
        if valid is not None:
            a = jnp.where(valid, a, 0.0)
        acc_ref[...] += _nt(a.astype(BF16), vt)

    @pl.when(jj == 0)
    def _():
        q = jnp.concatenate([q_ref[0].astype(F32)] * N_HEADS_A, axis=0)
        r = lax.broadcasted_iota(I32, q.shape, 0) // t_new
        cidx = lax.broadcasted_iota(I32, q.shape, 1) // HEAD_DIM
        qbd_ref[...] = jnp.where(r == cidx, q, 0.0).astype(BF16)
        c_ref[...] = jnp.zeros_like(c_ref)
        acc_ref[...] = jnp.zeros_like(acc_ref)
        tq = lax.broadcasted_iota(I32, (rows, LANES), 0) % t_new
        s = lax.broadcasted_iota(I32, (rows, LANES), 1)
        process([new_ref], s < tq)

    process(page_refs, None)

    @pl.when(jj == n_steps - 1)
    def _():
        acc = acc_ref[...]
        r = lax.broadcasted_iota(I32, acc.shape, 0) // t_new
        cidx = lax.broadcasted_iota(I32, acc.shape, 1) // HEAD_DIM
        acc = jnp.where(r == cidx, acc, 0.0)
        out = acc[0:t_new]
        for hd in range(1, N_HEADS_A):
            out = out + acc[hd * t_new:(hd + 1) * t_new]
        o_ref[0] = out


def _sba_sample(page_table, q, newt, cache_t, uu):
    nb, t_new, _ = q.shape
    n_pages = page_table.shape[1]
    n_steps = n_pages // PAGES_PER_STEP
    rows = N_HEADS_A * t_new

    def page_spec(i):
        def imap(b, j, pt):
            return (pt[b, n_pages - (j + 1) * PAGES_PER_STEP + i], 0, 0, 0, 0)
        return pl.BlockSpec((1, 2, N_HEADS_A, HEAD_DIM, LANES), imap)

    grid_spec = pltpu.PrefetchScalarGridSpec(
        num_scalar_prefetch=1,
        grid=(nb, n_steps),
        in_specs=[
            pl.BlockSpec((1, t_new, WIDTH_A), lambda b, j, pt: (b, 0, 0)),
            pl.BlockSpec((1, 2, N_HEADS_A, HEAD_DIM, LANES), lambda b, j, pt: (b, 0, 0, 0, 0)),
            *[page_spec(i) for i in range(PAGES_PER_STEP)],
            _const_spec(uu.shape),
        ],
        out_specs=pl.BlockSpec((1, t_new, WIDTH_A), lambda b, j, pt: (b, 0, 0)),
        scratch_shapes=[
            pltpu.VMEM((rows, LANES), F32),
            pltpu.VMEM((rows, WIDTH_A), F32),
            pltpu.VMEM((rows, WIDTH_A), BF16),
        ],
    )
    return pl.pallas_call(
        functools.partial(_sba_sample_kernel, n_steps=n_steps),
        grid_spec=grid_spec,
        out_shape=jax.ShapeDtypeStruct((nb, t_new, WIDTH_A), F32),
        compiler_params=_params("parallel", "arbitrary"),
        name="sba_sample",
    )(page_table, q, newt, *([cache_t] * PAGES_PER_STEP), uu)


def _select_blocks(imp, qpos, n_blocks, n_sel, block_axis):
    sb = lax.broadcasted_iota(I32, imp.shape, block_axis)
    imp = jnp.where(sb * L_SEL > qpos, NEG, imp)
    forced = (sb == 0) | (sb == qpos // L_SEL)
    imp = jnp.where(forced, SEL_BONUS, imp)
    rank = jnp.zeros(imp.shape, F32)
    for i in range(n_blocks):
        vi = imp[i:i + 1, :] if block_axis == 0 else imp[:, i:i + 1]
        beats = jnp.where(vi > imp, 1.0, jnp.where(vi == imp, jnp.where(sb > i, 1.0, 0.0), 0.0))
        rank = rank + beats
    return jnp.where((rank < n_sel) & (sb < n_blocks), 1.0, 0.0)


def _nsa_prompt_kernel(qt_ref, kcmp_ref, vcmpt_ref, kst_ref, vst_ref, kwt_ref, vwt_ref, gt_ref, e_ref, o_ref,
                       os_ref, *, span, n_var):
    g = pl.program_id(1)
    qi = pl.program_id(2)
    t = kst_ref.shape[2]
    var_keys = t // n_var
    ncb = kcmp_ref.shape[2]
    nsb = ncb // CMP_PER_SEL
    q0 = qi * QBLOCK
    tpos_l = q0 + lax.broadcasted_iota(I32, (1, QBLOCK), 1)
    nrow = GROUP_B * QBLOCK
    tpos_s = q0 + lax.broadcasted_iota(I32, (QBLOCK, 1), 0)
    tile = lambda x, axis: jnp.concatenate([x] * GROUP_B, axis=axis)
    q_all = jnp.concatenate(
        [qt_ref[0, m * HEAD_DIM:(m + 1) * HEAD_DIM].astype(F32).T for m in range(GROUP_B)], axis=0).astype(BF16)

    r = lax.broadcasted_iota(I32, (ncb, 1), 0)
    n = CMP_PER_SEL * (r % nsb) + r // nsb
    cvalid = ((n + 1) * L_CMP - 1) <= q0 + lax.broadcasted_iota(I32, (1, nrow), 1) % QBLOCK
    zc = jnp.where(cvalid, _nt(kcmp_ref[0, 0].astype(BF16), q_all) * SCALE, NEG)
    ex = jnp.exp(zc - jnp.max(zc, axis=0, keepdims=True))
    pc = ex / jnp.sum(ex, axis=0, keepdims=True)
    pc = jnp.where(cvalid, pc, 0.0)
    oc = _nn(vcmpt_ref[0, 0].astype(BF16), pc.astype(BF16))
    imp = pc[:, 0:QBLOCK]
    for m in range(1, GROUP_B):
        imp = imp + pc[:, m * QBLOCK:(m + 1) * QBLOCK]
    imp = imp[0:nsb] + imp[nsb:ncb]
    sel = _select_blocks(imp, tpos_l, nsb, min(N_SEL, nsb), 0).astype(BF16)

    q_scaled = (q_all.astype(F32) * SCALE).astype(BF16)

    def branch(kt, vt, bias):
        zm = _nn(q_scaled, kt) + tile(bias, 0)
        p = jnp.exp(zm - jnp.max(zm, axis=-1, keepdims=True)).astype(BF16)
        ones = jnp.ones((8, p.shape[1]), BF16)
        return _nt(vt, p) / _nt(ones, p)[0:1]

    for v in range(n_var):
        nk = (v + 1) * var_keys

        @pl.when(q0 // var_keys == v)
        def _():
            keymask = _tn(sel, e_ref[:, 0:nk])
            kpos = lax.broadcasted_iota(I32, (1, nk), 1)
            bias = jnp.where(keymask > 0.5, jnp.where(kpos <= tpos_s, 0.0, NEG), NEG)
            os_ref[...] = branch(kst_ref[0, :, 0:nk], vst_ref[0, :, 0:nk], bias)

    ws = pl.multiple_of(jnp.maximum(qi - WINDOW // QBLOCK, 0) * QBLOCK, QBLOCK)
    wpos = ws + lax.broadcasted_iota(I32, (1, span), 1)
    wbias = jnp.where(wpos <= tpos_s, jnp.where(wpos > tpos_s - WINDOW, 0.0, NEG), NEG)
    ow = branch(kwt_ref[0, :, pl.ds(ws, span)], vwt_ref[0, :, pl.ds(ws, span)], wbias)
    for m in range(GROUP_B):
        hb = g * GROUP_B + m
        gc = gt_ref[0, pl.ds(hb, 1), :]
        gs = gt_ref[0, pl.ds(N_HEADS_B + hb, 1), :]
        gw = gt_ref[0, pl.ds(2 * N_HEADS_B + hb, 1), :]
        cols = slice(m * QBLOCK, (m + 1) * QBLOCK)
        o_ref[0, m * HEAD_DIM:(m + 1) * HEAD_DIM] = gc * oc[:, cols] + gs * os_ref[:, cols] + gw * ow[:, cols]


def _nsa_prompt(qbt, kcmp, vcmpt, attb, gt, e):
    nb, _, t = qbt.shape
    nq = t // QBLOCK
    ncb = kcmp.shape[2]
    span = min(WINDOW + QBLOCK, t)
    n_var = 4 if nq % 4 == 0 else 1
    kv = lambda base, slot: pl.BlockSpec(
        (1, HEAD_DIM, t), lambda b, g, i: (b, base // HEAD_DIM + slot * N_KV_B + g, 0))
    return pl.pallas_call(
        functools.partial(_nsa_prompt_kernel, span=span, n_var=n_var),
        grid=(nb, N_KV_B, nq),
        in_specs=[
            pl.BlockSpec((1, GROUP_B * HEAD_DIM, QBLOCK), lambda b, g, i: (b, g, i)),
            pl.BlockSpec((1, 1, ncb, HEAD_DIM), lambda b, g, i: (b, g, 0, 0)),
            pl.BlockSpec((1, 1, HEAD_DIM, ncb), lambda b, g, i: (b, g, 0, 0)),
            kv(ATT_SEL, 0), kv(ATT_SEL, 1), kv(ATT_WIN, 0), kv(ATT_WIN, 1),
            pl.BlockSpec((1, LANES, QBLOCK), lambda b, g, i: (b, 0, i)),
            _const_spec(e.shape),
        ],
        out_specs=pl.BlockSpec((1, GROUP_B * HEAD_DIM, QBLOCK), lambda b, g, i: (b, g, i)),
        out_shape=jax.ShapeDtypeStruct((nb, WIDTH_B, t), F32),
        scratch_shapes=[pltpu.VMEM((HEAD_DIM, GROUP_B * QBLOCK), F32)],
        compiler_params=_params("parallel", "parallel", "arbitrary"),
        name="nsa_prompt",
    )(qbt, kcmp, vcmpt, attb, attb, attb, attb, gt, e)


def _nsa_sample_kernel(pt_ref, q_ref, new_ref, win_ref, wnew_ref, g_ref, *rest, n_steps, past):
    page_refs = rest[:PAGES_PER_STEP]
    wexp_ref, e_ref, perm_ref, o_ref, xs_ref, kst_ref, vst_ref, cmp_ref = rest[PAGES_PER_STEP:]
    jj = pl.program_id(1)
    t_new = q_ref.shape[1]
    cw = 2 * KV_WIDTH_B
    pair_rows = 2 * LANES // L_CMP

    for pr in range(PAGES_PER_STEP // 2):
        pages = jnp.concatenate([page_refs[2 * pr + i][0, 0:2].reshape(cw, LANES) for i in range(2)], axis=1)
        y = _nt(perm_ref[...], pages.astype(BF16))
        row0 = pl.multiple_of((jj * (PAGES_PER_STEP // 2) + pr) * pair_rows, pair_rows)
        for l in range(L_CMP):
            xs_ref[l, pl.ds(row0, pair_rows), :] = y[l * pair_rows:(l + 1) * pair_rows]
    for i, ref in enumerate(page_refs):
        off = pl.multiple_of((jj * PAGES_PER_STEP + i) * LANES, LANES)
        for g in range(N_KV_B):
            kst_ref[g, :, pl.ds(off, LANES)] = ref[0, 2, g].astype(BF16)
            vst_ref[g, :, pl.ds(off, LANES)] = ref[0, 3, g].astype(BF16)

    @pl.when(jj == n_steps - 1)
    def _():
        ncb = past // L_CMP
        nsb_past = past // L_SEL
        total = past + LANES
        for g in range(N_KV_B):
            kst_ref[g, :, past:total] = new_ref[0, 2, g].astype(BF16)
            vst_ref[g, :, past:total] = new_ref[0, 3, g].astype(BF16)

        halves = range(cw // LANES)

        def cbody(i, accs):
            out = []
            for par, acc in enumerate(accs):
                l = 2 * i + par
                w = wexp_ref[pl.ds(pl.multiple_of(l * cw, cw), cw), :]
                out.append(acc + _nn(xs_ref[l].astype(BF16), w))
            return tuple(out)

        zero = jnp.zeros((ncb, cw), F32)
        cmp = sum(lax.fori_loop(0, L_CMP // 2, cbody, (zero, zero)))
        for hf in halves:
            cmp_ref[hf] = cmp[:, hf * LANES:(hf + 1) * LANES]
        cmp_perm = jnp.concatenate(
            [jnp.concatenate([cmp_ref[hf, pl.ds(par, nsb_past, stride=CMP_PER_SEL), :] for hf in halves], axis=1)
             for par in range(CMP_PER_SEL)], axis=0)

        rows = GROUP_B * t_new
        tq = lax.broadcasted_iota(I32, (rows, 1), 0) % t_new
        qpos = past + tq
        qpos_t = past + lax.broadcasted_iota(I32, (t_new, 1), 0)
        r = lax.broadcasted_iota(I32, (1, ncb), 1)
        n = CMP_PER_SEL * (r % nsb_past) + r // nsb_past
        cvalid = ((n + 1) * L_CMP - 1) <= qpos
        kpos = lax.broadcasted_iota(I32, (1, total), 1)
        wlen = win_ref.shape[4]
        wl = lax.broadcasted_iota(I32, (1, wlen + LANES), 1)
        wpos = jnp.where(wl < wlen, past - wlen + wl, past + wl - wlen)
        tile = lambda x: jnp.concatenate([x] * GROUP_B, axis=0)
        wbias = tile(jnp.where(wpos <= qpos_t, jnp.where(wpos > qpos_t - WINDOW, 0.0, NEG), NEG))
        nsb = nsb_past + 1
        nsb_pad = e_ref.shape[0]
        gates = g_ref[0]

        qall = q_ref[0].astype(F32)
        qgs, ocs, sels = [], [], []
        for g in range(N_KV_B):
            qg = jnp.concatenate(
                [qall[:, (g * GROUP_B + m) * HEAD_DIM:(g * GROUP_B + m + 1) * HEAD_DIM] for m in range(GROUP_B)],
                axis=0).astype(BF16)
            qgs.append(qg)
            kcmp = cmp_perm[:, g * HEAD_DIM:(g + 1) * HEAD_DIM].astype(BF16)
            vcmp = cmp_perm[:, KV_WIDTH_B + g * HEAD_DIM:KV_WIDTH_B + (g + 1) * HEAD_DIM].astype(BF16)
            zc = jnp.where(cvalid, _nt(qg, kcmp) * SCALE, NEG)
            ex = jnp.exp(zc - jnp.max(zc, axis=-1, keepdims=True))
            pc = ex / jnp.sum(ex, axis=-1, keepdims=True)
            pc = jnp.where(cvalid, pc, 0.0)
            ocs.append(_nn(pc.astype(BF16), vcmp))
            imp = pc[0:t_new]
            for m in range(1, GROUP_B):
                imp = imp + pc[m * t_new:(m + 1) * t_new]
            imp = imp[:, 0:nsb_past] + imp[:, nsb_past:ncb]
            imp = jnp.concatenate([imp, jnp.zeros((t_new, nsb_pad - nsb_past), F32)], axis=1)
            sels.append(_select_blocks(imp, qpos_t, nsb, min(N_SEL, nsb), 1))

        keymasks = _nn(jnp.concatenate(sels, axis=0).astype(BF16), e_ref[...])
        for g in range(N_KV_B):
            qg = qgs[g]
            oc = ocs[g]
            keymask = keymasks[g * t_new:(g + 1) * t_new]
            sbias = tile(jnp.where(keymask > 0.5, jnp.where(kpos <= qpos_t, 0.0, NEG), NEG))

            def branch(kt, vt, bias):
                zm = _nn(qg, kt) * SCALE + bias
                p = jnp.exp(zm - jnp.max(zm, axis=-1, keepdims=True))
                l = jnp.sum(p, axis=-1, keepdims=True)
                return _nt(p.astype(BF16), vt) / l

            os_ = branch(kst_ref[g], vst_ref[g], sbias)
            kwt = jnp.concatenate([win_ref[0, 0, g], wnew_ref[0, 0, g]], axis=1).astype(BF16)
            vwt = jnp.concatenate([win_ref[0, 1, g], wnew_ref[0, 1, g]], axis=1).astype(BF16)
            ow = branch(kwt, vwt, wbias)
            for m in range(GROUP_B):
                hb = g * GROUP_B + m
                sl = slice(m * t_new, (m + 1) * t_new)
                o_ref[0, :, hb * HEAD_DIM:(hb + 1) * HEAD_DIM] = (
                    gates[:, hb:hb + 1] * oc[sl]
                    + gates[:, N_HEADS_B + hb:N_HEADS_B + hb + 1] * os_[sl]
                    + gates[:, 2 * N_HEADS_B + hb:2 * N_HEADS_B + hb + 1] * ow[sl])


def _pair_permutation():
    blocks = LANES // L_CMP
    l, page, n = np.meshgrid(np.arange(L_CMP), np.arange(2), np.arange(blocks), indexing="ij")
    perm = np.zeros((2 * LANES, 2 * LANES), np.float32)
    perm[(l * 2 * blocks + page * blocks + n).ravel(), (page * LANES + n * L_CMP + l).ravel()] = 1.0
    return jnp.asarray(perm, dtype=BF16)


def _nsa_sample(page_table, q, newt, win_t, wnew_t, gates, cache_t, wexp, e):
    nb, t_new, _ = q.shape
    perm = _pair_permutation()
    n_pages = page_table.shape[1]
    n_steps = n_pages // PAGES_PER_STEP
    past = n_pages * LANES
    wlen = win_t.shape[4]
    cw = 2 * KV_WIDTH_B

    def page_spec(i):
        def imap(b, j, pt):
            return (pt[b, j * PAGES_PER_STEP + i], 0, 0, 0, 0)
        return pl.BlockSpec((1, 4, N_KV_B, HEAD_DIM, LANES), imap)

    grid_spec = pltpu.PrefetchScalarGridSpec(
        num_scalar_prefetch=1,
        grid=(nb, n_steps),
        in_specs=[
            pl.BlockSpec((1, t_new, WIDTH_B), lambda b, j, pt: (b, 0, 0)),
            pl.BlockSpec((1, 4, N_KV_B, HEAD_DIM, LANES), lambda b, j, pt: (b, 0, 0, 0, 0)),
            pl.BlockSpec((1, 2, N_KV_B, HEAD_DIM, wlen), lambda b, j, pt: (b, 0, 0, 0, 0)),
            pl.BlockSpec((1, 2, N_KV_B, HEAD_DIM, LANES), lambda b, j, pt: (b, 0, 0, 0, 0)),
            pl.BlockSpec((1, t_new, LANES), lambda b, j, pt: (b, 0, 0)),
            *[page_spec(i) for i in range(PAGES_PER_STEP)],
            _const_spec(wexp.shape),
            _const_spec(e.shape),
            _const_spec(perm.shape),
        ],
        out_specs=pl.BlockSpec((1, t_new, WIDTH_B), lambda b, j, pt: (b, 0, 0)),
        scratch_shapes=[
            pltpu.VMEM((L_CMP, past // L_CMP, cw), F32),
            pltpu.VMEM((N_KV_B, HEAD_DIM, past + LANES), BF16),
            pltpu.VMEM((N_KV_B, HEAD_DIM, past + LANES), BF16),
            pltpu.VMEM((cw // LANES, past // L_CMP, LANES), F32),
        ],
    )
    return pl.pallas_call(
        functools.partial(_nsa_sample_kernel, n_steps=n_steps, past=past),
        grid_spec=grid_spec,
        out_shape=jax.ShapeDtypeStruct((nb, t_new, WIDTH_B), F32),
        compiler_params=_params("parallel", "arbitrary"),
        name="nsa_sample",
    )(page_table, q, newt, win_t, wnew_t, gates, *([cache_t] * PAGES_PER_STEP), wexp, e, perm)


def _combine_kernel(x_ref, oat_ref, obt_ref, na_ref, nb_ref, wout_ref, nm_ref, wup_ref, wdown_ref, y_ref):
    def norm_t(o, gcol):
        return o * lax.rsqrt(jnp.mean(o * o, axis=0, keepdims=True) + EPS) * gcol

    mixed_t = jnp.concatenate([norm_t(oat_ref[0], na_ref[...]), norm_t(obt_ref[0], nb_ref[...])], axis=0)
    x1 = x_ref[0] + _tn(mixed_t.astype(BF16), wout_ref[...])
    h = x1 * lax.rsqrt(jnp.mean(x1 * x1, axis=-1, keepdims=True) + EPS) * nm_ref[...]
    up = jnp.square(jnp.maximum(_nn(h.astype(BF16), wup_ref[...]), 0.0))
    y_ref[0] = x1 + _nn(up.astype(BF16), wdown_ref[...])


def _combine(x, oat, obt, na_col, nb_col, wout, nm, wup, wdown, tile):
    nb, t, d = x.shape
    return pl.pallas_call(
        _combine_kernel,
        grid=(nb, t // tile),
        in_specs=[
            pl.BlockSpec((1, tile, d), lambda b, i: (b, i, 0)),
            pl.BlockSpec((1, WIDTH_A, tile), lambda b, i: (b, 0, i)),
            pl.BlockSpec((1, WIDTH_B, tile), lambda b, i: (b, 0, i)),
            _const_spec(na_col.shape), _const_spec(nb_col.shape), _const_spec(wout.shape),
            _const_spec(nm.shape), _const_spec(wup.shape), _const_spec(wdown.shape),
        ],
        out_specs=pl.BlockSpec((1, tile, d), lambda b, i: (b, i, 0)),
        out_shape=jax.ShapeDtypeStruct((nb, t, d), F32),
        compiler_params=_params("parallel", "parallel"),
        name="combine_mlp",
    )(x, oat, obt, na_col, nb_col, wout, nm, wup, wdown)


def _expanded_cmp_weights(w_cmp_k, w_cmp_v):
    wk = w_cmp_k.reshape(L_CMP, HEAD_DIM, HEAD_DIM)
    wv = w_cmp_v.reshape(L_CMP, HEAD_DIM, HEAD_DIM)
    slots = [wk] * N_KV_B + [wv] * N_KV_B
    ns = len(slots)
    zero = jnp.zeros_like(wk)
    rows = [jnp.concatenate([slots[i] if i == j else zero for j in range(ns)], axis=2) for i in range(ns)]
    return jnp.concatenate(rows, axis=1).reshape(L_CMP * ns * HEAD_DIM, ns * HEAD_DIM).astype(BF16)


def _expansion(n_rows, n_keys):
    j = np.arange(n_rows)[:, None]
    s = np.arange(n_keys)[None, :]
    return jnp.asarray(s // L_SEL == j, dtype=BF16)


def _kv_out(xt, n_slots, n_heads):
    b, _, t = xt.shape
    return xt.reshape(b, n_slots, n_heads, HEAD_DIM, t).transpose(0, 4, 1, 2, 3)[None]


def kernel(x_prompt, x_sample, cache_sba_kv, cache_nsa_kv, state_win_kv, page_table, norm_attn, w_in, b_gate,
           q_norm_b, k_norm_b, w_cmp_k, w_cmp_v, norm_out_a, norm_out_b, w_out, norm_mlp, w_up, w_down):
    assert norm_attn.shape[0] == 1, "single trunk layer"
    nb, t, d = x_prompt.shape
    db, t_new, _ = x_sample.shape
    n_pages = page_table.shape[1]
    past = n_pages * LANES
    assert cache_sba_kv.shape[2] == LANES and t % QBLOCK == 0 and t_new < L_CMP
    assert n_pages % PAGES_PER_STEP == 0 and t >= WINDOW and state_win_kv.shape[2] == WINDOW

    wt = jnp.pad(jnp.transpose(w_in[0]), ((0, PROJ_ROWS - w_in.shape[2]), (0, 0))).astype(BF16)
    bg_col = jnp.pad(b_gate[0], (0, LANES - N_GATE))[:, None]
    qn_col = q_norm_b[0][:, None]
    kn_cols = k_norm_b[0][:, :, None]
    wexp = _expanded_cmp_weights(w_cmp_k[0], w_cmp_v[0])
    na_col = norm_out_a[0][:, None]
    nb_col = norm_out_b[0][:, None]
    wout = w_out[0].astype(BF16)
    wup = w_up[0].astype(BF16)
    wdown = w_down[0].astype(BF16)
    sidx = np.arange(LANES)
    u2 = np.concatenate([sidx[:, None] > sidx[None, :], np.ones((LANES, LANES), bool)], axis=1)
    uu = jnp.asarray(np.concatenate([u2, u2], axis=0), dtype=BF16)

    tile = min(256, t)
    qat, sbat, qbt, nsat, wint, gt, cmpin, attb = _project(
        x_prompt, jnp.arange(t), norm_attn, wt, bg_col, qn_col, kn_cols, tile)
    ncb = t // L_CMP
    nsb = ncb // CMP_PER_SEL
    cw = 2 * KV_WIDTH_B
    xflat = cmpin.reshape(nb, nsb, CMP_PER_SEL, L_CMP * cw).transpose(0, 2, 1, 3).reshape(nb * ncb, L_CMP * cw)
    cmp = _compress(xflat, wexp).reshape(nb, ncb, 2 * N_KV_B, HEAD_DIM)
    kcmp = cmp[:, :, 0:N_KV_B].transpose(0, 2, 1, 3)
    vcmpt = cmp[:, :, N_KV_B:].transpose(0, 2, 3, 1)
    oat = _sba_prompt(qat, attb, uu)
    obt = _nsa_prompt(qbt, kcmp, vcmpt, attb, gt, _expansion(nsb, t))
    y_prompt = _combine(x_prompt, oat, obt, na_col, nb_col, wout, norm_mlp, wup, wdown, tile)

    cols = db * t_new
    pos_s = past + jnp.arange(cols) % t_new
    qat_s, sbat_s, qbt_s, nsat_s, wint_s, gt_s, _, _ = _project(
        x_sample.reshape(1, cols, d), pos_s, norm_attn, wt, bg_col, qn_col, kn_cols, min(256, cols))

    def per_batch_rows(xt):
        return xt[0].reshape(xt.shape[1], db, t_new).transpose(1, 2, 0)

    def per_batch_padded(xt, n_slots, n_heads):
        x5 = xt[0].reshape(n_slots, n_heads, HEAD_DIM, db, t_new).transpose(3, 0, 1, 2, 4)
        return jnp.pad(x5, ((0, 0),) * 4 + ((0, LANES - t_new),))

    sba_cache_t = jnp.transpose(cache_sba_kv[0], (0, 2, 3, 4, 1))
    nsa_cache_t = jnp.transpose(cache_nsa_kv[0], (0, 2, 3, 4, 1))
    win_t = jnp.transpose(state_win_kv[0], (0, 2, 3, 4, 1))

    oa_s = _sba_sample(page_table, per_batch_rows(qat_s), per_batch_padded(sbat_s, 2, N_HEADS_A), sba_cache_t, uu)
    nsb_pad = -(-(past // L_SEL + 1) // LANES) * LANES
    ob_s = _nsa_sample(page_table, per_batch_rows(qbt_s), per_batch_padded(nsat_s, 4, N_KV_B), win_t,
                       per_batch_padded(wint_s, 2, N_KV_B), per_batch_rows(gt_s), nsa_cache_t, wexp,
                       _expansion(nsb_pad, past + LANES))
    to_t = lambda o: o.reshape(cols, o.shape[2]).T[None]
    y_sample = _combine(x_sample.reshape(1, cols, d), to_t(oa_s), to_t(ob_s), na_col, nb_col, wout, norm_mlp,
                        wup, wdown, min(256, cols)).reshape(db, t_new, d)

    def sample_kv(xt, n_slots, n_heads):
        return xt[0].reshape(n_slots, n_heads, HEAD_DIM, db, t_new).transpose(3, 4, 0, 1, 2)[None]

    win_new = sample_kv(wint_s, 2, N_KV_B)
    win_s = jnp.concatenate([state_win_kv[:, :, t_new:], win_new], axis=2)
    return (y_prompt, y_sample, _kv_out(sbat, 2, N_HEADS_A), _kv_out(nsat, 4, N_KV_B),
            _kv_out(wint[:, :, t - WINDOW:], 2, N_KV_B), sample_kv(sbat_s, 2, N_HEADS_A),
            sample_kv(nsat_s, 4, N_KV_B), win_s)
```
